```python
import math
import jax, jax.numpy as jnp
from jax import lax
import numpy as np

D_MODEL = 1024
BATCH = 8
SEQ = 8192
DEPTH = 2

PLE_DIM = 256

GRID_W = 64
NA_HEADS = 8
NA_HEAD_DIM = 64
NA_ROWS = 8
NA_COLS = 16
NA_BAND = 2 * NA_COLS
NA_NCB = GRID_W // NA_COLS
NA_WIDTH = NA_HEADS * NA_HEAD_DIM

MLA_HEADS = 8
MLA_NOPE = 64
MLA_ROPE = 32
MLA_V = 64
MLA_Q_LORA = 768
MLA_KV_LORA = 256
MLA_WIDTH = MLA_HEADS * MLA_V
ROPE_THETA = 10000.0
Q_BLOCK = 128

N_IN = 3 * NA_WIDTH + MLA_Q_LORA + MLA_KV_LORA + MLA_ROPE + 2 * D_MODEL

D_FF = 2816
N_EXPERTS = 8
TOP_K = 2
D_FF_EXPERT = 3584
N_DENSE = (DEPTH + 1) // 2
N_MOE = DEPTH // 2

DEEPNORM_ALPHA = (2 * DEPTH) ** 0.25
DEEPNORM_BETA = (8 * DEPTH) ** -0.25
LN_EPS = 1e-5
RMS_EPS = 1e-6
MASK_VALUE = -1e30

kernel_name = 'hybrid_natten_mla_moe_deepnorm_encoder'


def _layer_norm(x, g, b):
    xf = x.astype(jnp.float32)
    mu = jnp.mean(xf, axis=-1, keepdims=True)
    var = jnp.mean(jnp.square(xf - mu), axis=-1, keepdims=True)
    y = (xf - mu) * lax.rsqrt(var + LN_EPS)
    return (y * g.astype(jnp.float32) + b.astype(jnp.float32)).astype(x.dtype)


def _rms_norm(x, g):
    xf = x.astype(jnp.float32)
    y = xf * lax.rsqrt(jnp.mean(jnp.square(xf), axis=-1, keepdims=True) + RMS_EPS)
    return (y * g.astype(jnp.float32)).astype(x.dtype)


def _rope(x, cos, sin):
    half = x.shape[-1] // 2
    x1, x2 = x[..., :half], x[..., half:]
    return jnp.concatenate([x1 * cos - x2 * sin, x2 * cos + x1 * sin], axis=-1)


def _swiglu(x, w1, w3, w2):
    return (jax.nn.silu(x @ w1) * (x @ w3)) @ w2


def _na_column_structure():
    band_start = np.clip(np.arange(NA_NCB) * NA_COLS - NA_COLS // 2, 0, GRID_W - NA_BAND)
    band_idx = band_start[:, None] + np.arange(NA_BAND)[None, :]
    q_col = np.arange(GRID_W).reshape(NA_NCB, NA_COLS)
    win_start = np.clip(q_col - NA_COLS // 2, 0, GRID_W - NA_COLS)
    k_col = band_idx[:, None, :]
    valid = (k_col >= win_start[..., None]) & (k_col < win_start[..., None] + NA_COLS)
    dc_idx = np.clip(k_col - q_col[..., None] + NA_COLS - 1, 0, 2 * NA_COLS - 2)
    return band_idx.astype(np.int32), valid, dc_idx.astype(np.int32)


def neighborhood_attention(q, k, v, rpb):
    b, s, h, dh = q.shape
    rows = s // GRID_W
    kr = min(NA_ROWS, rows)
    band_idx, valid, dc_idx = _na_column_structure()
    valid = jnp.asarray(valid)[:, :, None, :]
    col_bias = rpb[:, :, dc_idx]
    kg = k.reshape(b, rows, GRID_W, h, dh)
    vg = v.reshape(b, rows, GRID_W, h, dh)
    qg = q.reshape(b, rows, NA_NCB, NA_COLS, h, dh).transpose(1, 0, 2, 3, 4, 5)
    scale = dh ** -0.5

    def row_block(args):
        r, q_row = args
        rs = jnp.clip(r - kr // 2, 0, rows - kr)
        k_band = jnp.take(lax.dynamic_slice_in_dim(kg, rs, kr, axis=1), band_idx, axis=2)
        v_band = jnp.take(lax.dynamic_slice_in_dim(vg, rs, kr, axis=1), band_idx, axis=2)
        dr_idx = rs + jnp.arange(kr, dtype=jnp.int32) - r + (NA_ROWS - 1)
        bias = jnp.take(col_bias, dr_idx, axis=1).transpose(0, 2, 3, 1, 4)
        sc = jnp.einsum('bnqhd,bknjhd->bhnqkj', q_row, k_band).astype(jnp.float32) * scale
        sc = jnp.where(valid, sc + bias.astype(jnp.float32), MASK_VALUE)
        pr = jax.nn.softmax(sc.reshape(sc.shape[:4] + (kr * NA_BAND,)), axis=-1).reshape(sc.shape)
        o = jnp.einsum('bhnqkj,bknjhd->bnqhd', pr.astype(v.dtype), v_band)
        return o.reshape(b, GRID_W, h * dh)

    out = lax.map(row_block, (jnp.arange(rows, dtype=jnp.int32), qg))
    return out.transpose(1, 0, 2, 3).reshape(b, s, h * dh)


def latent_attention(q, k, v):
    b, s, h, dq = q.shape
    nb = s // Q_BLOCK
    scale = dq ** -0.5
    qb = q.reshape(b, nb, Q_BLOCK, h, dq).transpose(1, 0, 2, 3, 4)

    def block(qi):
        sc = jnp.einsum('bqhd,bkhd->bhqk', qi, k).astype(jnp.float32) * scale
        pr = jax.nn.softmax(sc, axis=-1)
        return jnp.einsum('bhqk,bkhd->bqhd', pr.astype(v.dtype), v)

    o = lax.map(block, qb)
    return o.transpose(1, 0, 2, 3, 4).reshape(b, s, h * v.shape[-1])


def moe_swiglu(x, w_router, b_router, w1, w3, w2):
    logits = (x @ w_router).astype(jnp.float32) + b_router.astype(jnp.float32)
    top_vals, top_idx = lax.top_k(logits, TOP_K)
    top_w = jax.nn.softmax(top_vals, axis=-1)
    gates = jnp.sum(jax.nn.one_hot(top_idx, N_EXPERTS, dtype=jnp.float32) * top_w[..., None], axis=-2)
    out = jnp.zeros_like(x)
    for e in range(N_EXPERTS):
        out = out + gates[..., e:e + 1].astype(x.dtype) * _swiglu(x, w1[e], w3[e], w2[e])
    return out


def setup_inputs(seed: int = 0) -> dict:
    key = jax.random.key(seed)
    ks = jax.random.split(key, 26)
    f32 = jnp.float32

    def nrm(k, shape, fan_in, scale=1.0):
        return jax.random.normal(k, shape, f32) * (scale * fan_in ** -0.5)

    def gain(k, shape):
        return 1.0 + 0.05 * jax.random.normal(k, shape, f32)

    def bias(k, shape, s=0.02):
        return s * jax.random.normal(k, shape, f32)

    beta = DEEPNORM_BETA
    return {
        'x': jax.random.normal(ks[0], (BATCH, SEQ, D_MODEL), f32),
        'p': jax.random.normal(ks[1], (DEPTH, BATCH, SEQ, PLE_DIM), f32),
        'w_in': nrm(ks[2], (DEPTH, D_MODEL, N_IN), D_MODEL),
        'b_gate': bias(ks[3], (DEPTH, 2 * D_MODEL)),
        'q_norm_g': gain(ks[4], (DEPTH, MLA_Q_LORA)),
        'w_uq': nrm(ks[5], (DEPTH, MLA_Q_LORA, MLA_HEADS * (MLA_NOPE + MLA_ROPE)), MLA_Q_LORA),
        'kv_norm_g': gain(ks[6], (DEPTH, MLA_KV_LORA)),
        'w_ukv': nrm(ks[7], (DEPTH, MLA_KV_LORA, MLA_HEADS * (MLA_NOPE + MLA_V)), MLA_KV_LORA),
        'na_rpb': bias(ks[8], (DEPTH, NA_HEADS, 2 * NA_ROWS - 1, 2 * NA_COLS - 1), 0.1),
        'w_na_o': nrm(ks[9], (DEPTH, NA_WIDTH, D_MODEL), NA_WIDTH, beta),
        'w_mla_o': nrm(ks[10], (DEPTH, MLA_WIDTH, D_MODEL), MLA_WIDTH, beta),
        'w_out': nrm(ks[11], (DEPTH, D_MODEL, D_MODEL), D_MODEL, beta),
        'ln1_g': gain(ks[12], (DEPTH, D_MODEL)),
        'ln1_b': bias(ks[13], (DEPTH, D_MODEL)),
        'ffn_w1': nrm(ks[14], (N_DENSE, D_MODEL, D_FF), D_MODEL),
        'ffn_w3': nrm(ks[15], (N_DENSE, D_MODEL, D_FF), D_MODEL),
        'ffn_w2': nrm(ks[16], (N_DENSE, D_FF, D_MODEL), D_FF, beta),
        'moe_w_router': nrm(ks[17], (N_MOE, D_MODEL, N_EXPERTS), D_MODEL),
        'moe_b_router': bias(ks[18], (N_MOE, N_EXPERTS), 0.01),
        'moe_w1': nrm(ks[19], (N_MOE, N_EXPERTS, D_MODEL, D_FF_EXPERT), D_MODEL),
        'moe_w3': nrm(ks[20], (N_MOE, N_EXPERTS, D_MODEL, D_FF_EXPERT), D_MODEL),
        'moe_w2': nrm(ks[21], (N_MOE, N_EXPERTS, D_FF_EXPERT, D_MODEL), D_FF_EXPERT, beta),
        'w_ple_gate': nrm(ks[22], (DEPTH, D_MODEL, D_MODEL), D_MODEL),
        'w_ple': nrm(ks[23], (DEPTH, PLE_DIM, D_MODEL), PLE_DIM, beta),
        'ln2_g': gain(ks[24], (DEPTH, D_MODEL)),
        'ln2_b': bias(ks[25], (DEPTH, D_MODEL)),
    }


def reference(x, p, w_in, b_gate, q_norm_g, w_uq, kv_norm_g, w_ukv, na_rpb, w_na_o, w_mla_o,
              w_out, ln1_g, ln1_b, ffn_w1, ffn_w3, ffn_w2, moe_w_router, moe_b_router,
              moe_w1, moe_w3, moe_w2, w_ple_gate, w_ple, ln2_g, ln2_b):
    b, s, _ = x.shape
    pos = jnp.arange(s, dtype=jnp.float32)
    inv_freq = ROPE_THETA ** (-jnp.arange(0, MLA_ROPE // 2, dtype=jnp.float32) * (2.0 / MLA_ROPE))
    ang = pos[:, None] * inv_freq[None, :]
    cos = jnp.cos(ang)[:, None, :].astype(x.dtype)
    sin = jnp.sin(ang)[:, None, :].astype(x.dtype)
    splits = list(np.cumsum([3 * NA_WIDTH, MLA_Q_LORA, MLA_KV_LORA, MLA_ROPE, D_MODEL]))

    for i in range(DEPTH):
        proj = x @ w_in[i]
        na_qkv, q_lat, kv_lat, k_rope, g_na, g_mla = jnp.split(proj, [int(o) for o in splits], axis=-1)

        q_na, k_na, v_na = jnp.split(na_qkv.reshape(b, s, 3, NA_HEADS, NA_HEAD_DIM), 3, axis=2)
        o_na = neighborhood_attention(q_na[:, :, 0], k_na[:, :, 0], v_na[:, :, 0], na_rpb[i])

        q_mla = (_rms_norm(q_lat, q_norm_g[i]) @ w_uq[i]).reshape(b, s, MLA_HEADS, MLA_NOPE + MLA_ROPE)
        q_mla = jnp.concatenate([q_mla[..., :MLA_NOPE], _rope(q_mla[..., MLA_NOPE:], cos, sin)], axis=-1)
        kv = (_rms_norm(kv_lat, kv_norm_g[i]) @ w_ukv[i]).reshape(b, s, MLA_HEADS, MLA_NOPE + MLA_V)
        k_r = jnp.broadcast_to(_rope(k_rope[:, :, None, :], cos, sin), (b, s, MLA_HEADS, MLA_ROPE))
        k_mla = jnp.concatenate([kv[..., :MLA_NOPE], k_r], axis=-1)
        o_mla = latent_attention(q_mla, k_mla, kv[..., MLA_NOPE:])

        merged = (jax.nn.sigmoid(g_na + b_gate[i, :D_MODEL]) * (o_na @ w_na_o[i])
                  + jax.nn.sigmoid(g_mla + b_gate[i, D_MODEL:]) * (o_mla @ w_mla_o[i]))
        x = _layer_norm(DEEPNORM_ALPHA * x + merged @ w_out[i], ln1_g[i], ln1_b[i])

        if i % 2 == 0:
            j = i // 2
            f = _swiglu(x, ffn_w1[j], ffn_w3[j], ffn_w2[j])
        else:
            j = i // 2
            f = moe_swiglu(x, moe_w_router[j], moe_b_router[j], moe_w1[j], moe_w3[j], moe_w2[j])

        ple = jax.nn.sigmoid(x @ w_ple_gate[i]) * (p[i] @ w_ple[i])
        x = _layer_norm(DEEPNORM_ALPHA * x + f + ple, ln2_g[i], ln2_b[i])
    return x
```

```python
import functools
import math

import numpy as np
import jax
import jax.numpy as jnp
from jax import lax
from jax.experimental import pallas as pl
from jax.experimental.pallas import tpu as pltpu

BF = jnp.bfloat16
F32 = jnp.float32

GRID_W = 64
NA_HEADS = 8
NA_HEAD_DIM = 64
NA_ROWS = 8
NA_COLS = 16
NA_WIDTH = NA_HEADS * NA_HEAD_DIM
MLA_HEADS = 8
MLA_NOPE = 64
MLA_ROPE = 32
MLA_V = 64
MLA_Q_LORA = 768
MLA_KV_LORA = 256
ROPE_THETA = 10000.0
N_EXPERTS = 8
LN_EPS = 1e-5
RMS_EPS = 1e-6
MASK_VALUE = -1e30

LANES = 128
BF16_SUBLANES = 16
VMEM_LIMIT_BYTES = 56 * 1024 * 1024

HEAD_PAD = LANES
VT_ROWS = MLA_V + BF16_SUBLANES


def _dot(a, b):
    return jnp.dot(a, b, preferred_element_type=F32)


def _dot_nt(a, b):
    return lax.dot_general(a, b, (((1,), (1,)), ((), ())), preferred_element_type=F32)


def _dot_tn(a, b):
    return lax.dot_general(a, b, (((0,), (0,)), ((), ())), preferred_element_type=F32)


def _const_spec(shape):
    nd = len(shape)
    return pl.BlockSpec(shape, lambda *_: (0,) * nd, pipeline_mode=pl.Buffered(1))


def _params(sem):
    return pltpu.CompilerParams(dimension_semantics=sem, vmem_limit_bytes=VMEM_LIMIT_BYTES)


def _tiles(n_tokens, seq):
    return dict(
        tm=min(512, seq),
        tq=min(512, seq),
        tk=min(512, seq),
        na_rows=8,
        moe_tm=1024 if n_tokens >= 8192 else 128,
        moe_chunks=4,
        ffn_chunks=2,
    )


def _layer_norm(y, g, b):
    mu = jnp.mean(y, axis=-1, keepdims=True)
    yc = y - mu
    var = jnp.mean(yc * yc, axis=-1, keepdims=True)
    return yc * lax.rsqrt(var + LN_EPS) * g + b


def _rms_norm(y, g):
    return y * lax.rsqrt(jnp.mean(y * y, axis=-1, keepdims=True) + RMS_EPS) * g


def _silu(y):
    return y * jax.nn.sigmoid(y)


def _proj_kernel(x_ref, wna_ref, wg_ref, bg_ref, wql_ref, wkvl_ref, wkr_ref, gq_ref, gkv_ref,
                 wuq_ref, wuqr_ref, wuk_ref, wuvt_ref, cq_ref, sq_ref, ck_ref, sk_ref,
                 qkv_ref, gate_ref, q_ref, k_ref, vt_ref):
    xb = x_ref[0].astype(BF)
    qkv_ref[0] = _dot(xb, wna_ref[...]).astype(BF)
    gate_ref[0] = jax.nn.sigmoid(_dot(xb, wg_ref[...]) + bg_ref[...]).astype(BF)

    qn = _rms_norm(_dot(xb, wql_ref[...]), gq_ref[...]).astype(BF)
    qa = _dot(qn, wuq_ref[...])
    qr = _dot(qn, wuqr_ref[...])
    cq, sq = cq_ref[...], sq_ref[...]
    for h in range(MLA_HEADS):
        sl = slice(h * HEAD_PAD, (h + 1) * HEAD_PAD)
        q_ref[0, h] = (qa[:, sl] * cq + qr[:, sl] * sq).astype(BF)

    kvn = _rms_norm(_dot(xb, wkvl_ref[...]), gkv_ref[...]).astype(BF)
    kr2 = _dot(xb, wkr_ref[...])
    kr = kr2[:, :HEAD_PAD] * ck_ref[...] + kr2[:, HEAD_PAD:] * sk_ref[...]
    kn = _dot(kvn, wuk_ref[...])
    for h in range(MLA_HEADS):
        sl = slice(h * HEAD_PAD, (h + 1) * HEAD_PAD)
        k_ref[0, h] = (kn[:, sl] + kr).astype(BF)

    vt = _dot_nt(wuvt_ref[...], kvn)
    ones = jnp.ones((BF16_SUBLANES, vt.shape[1]), BF)
    for h in range(MLA_HEADS):
        vt_ref[0, h, 0:MLA_V, :] = vt[h * MLA_V:(h + 1) * MLA_V].astype(BF)
        vt_ref[0, h, MLA_V:VT_ROWS, :] = ones


def _proj_call(x, w, tabs, tm):
    b, s, d = x.shape
    grid = (b, s // tm)
    weights = [w['wna'], w['wg'], w['bg'], w['wql'], w['wkvl'], w['wkr'], w['gq'], w['gkv'],
               w['wuq'], w['wuqr'], w['wuk'], w['wuvt']]
    tab_spec = pl.BlockSpec((tm, HEAD_PAD), lambda i, j: (j, 0))
    in_specs = ([pl.BlockSpec((1, tm, d), lambda i, j: (i, j, 0))]
                + [_const_spec(a.shape) for a in weights] + [tab_spec] * 4)
    out_shape = (
        jax.ShapeDtypeStruct((b, s, 3 * NA_WIDTH), BF),
        jax.ShapeDtypeStruct((b, s, 2 * d), BF),
        jax.ShapeDtypeStruct((b, MLA_HEADS, s, HEAD_PAD), BF),
        jax.ShapeDtypeStruct((b, MLA_HEADS, s, HEAD_PAD), BF),
        jax.ShapeDtypeStruct((b, MLA_HEADS, VT_ROWS, s), BF),
    )
    out_specs = (
        pl.BlockSpec((1, tm, 3 * NA_WIDTH), lambda i, j: (i, j, 0)),
        pl.BlockSpec((1, tm, 2 * d), lambda i, j: (i, j, 0)),
        pl.BlockSpec((1, MLA_HEADS, tm, HEAD_PAD), lambda i, j: (i, 0, j, 0)),
        pl.BlockSpec((1, MLA_HEADS, tm, HEAD_PAD), lambda i, j: (i, 0, j, 0)),
        pl.BlockSpec((1, MLA_HEADS, VT_ROWS, tm), lambda i, j: (i, 0, 0, j)),
    )
    return pl.pallas_call(
        _proj_kernel, grid=grid, in_specs=in_specs, out_specs=out_specs, out_shape=out_shape,
        compiler_params=_params(("parallel", "parallel")), name="proj",
    )(x, *weights, *tabs)


def _na_kernel(q_ref, kp_ref, kc_ref, kn_ref, vp_ref, vc_ref, vn_ref, bias_ref, o_ref,
               kcat_ref, vcat_ref, *, rows, rb):
    i = pl.program_id(1)
    blk = rb * GRID_W
    win = NA_ROWS * GRID_W
    kcat_ref[0:blk] = kp_ref[0]
    kcat_ref[blk:2 * blk] = kc_ref[0]
    kcat_ref[2 * blk:3 * blk] = kn_ref[0]
    vcat_ref[0:blk] = vp_ref[0]
    vcat_ref[blk:2 * blk] = vc_ref[0]
    vcat_ref[2 * blk:3 * blk] = vn_ref[0]
    lane = lax.broadcasted_iota(jnp.int32, (GRID_W, LANES), 1)
    first_head = lane < NA_HEAD_DIM

    def row(a, carry):
        r = i * rb + a
        rs = jnp.clip(r - NA_ROWS // 2, 0, rows - NA_ROWS)
        off = pl.multiple_of((rs - (i - 1) * rb) * GRID_W, GRID_W)
        d = rs - r + (NA_ROWS - 1)
        qoff = pl.multiple_of(a * GRID_W, GRID_W)
        qrow = q_ref[0, pl.ds(qoff, GRID_W), :]
        for hp in range(NA_HEADS // 2):
            sl = slice(hp * LANES, (hp + 1) * LANES)
            kw = kcat_ref[pl.ds(off, win), sl]
            vw = vcat_ref[pl.ds(off, win), sl]
            qp = qrow[:, sl]
            outs = []
            for hh in range(2):
                keep = first_head if hh == 0 else jnp.logical_not(first_head)
                qm = jnp.where(keep, qp, jnp.zeros_like(qp))
                sc = _dot_nt(qm, kw) + bias_ref[d, hp * 2 + hh]
                m = jnp.max(sc, axis=-1, keepdims=True)
                p = jnp.exp(sc - m)
                l = jnp.sum(p, axis=-1, keepdims=True)
                outs.append(_dot(p.astype(BF), vw) / l)
            o_ref[0, pl.ds(qoff, GRID_W), sl] = jnp.where(first_head, outs[0], outs[1]).astype(BF)
        return carry

    lax.fori_loop(0, rb, row, 0)


def _na_call(qkv, bias_tab, rb):
    b, s, _ = qkv.shape
    rows = s // GRID_W
    nb = rows // rb
    blk = rb * GRID_W

    def kv_spec(col, shift):
        return pl.BlockSpec((1, blk, NA_WIDTH),
                            lambda i, j: (i, jnp.clip(j + shift, 0, nb - 1), col))

    in_specs = [pl.BlockSpec((1, blk, NA_WIDTH), lambda i, j: (i, j, 0)),
                kv_spec(1, -1), kv_spec(1, 0), kv_spec(1, 1),
                kv_spec(2, -1), kv_spec(2, 0), kv_spec(2, 1),
                _const_spec(bias_tab.shape)]
    return pl.pallas_call(
        functools.partial(_na_kernel, rows=rows, rb=rb),
        grid=(b, nb), in_specs=in_specs,
        out_specs=pl.BlockSpec((1, blk, NA_WIDTH), lambda i, j: (i, j, 0)),
        out_shape=jax.ShapeDtypeStruct((b, s, NA_WIDTH), BF),
        scratch_shapes=[pltpu.VMEM((3 * blk, NA_WIDTH), BF), pltpu.VMEM((3 * blk, NA_WIDTH), BF)],
        compiler_params=_params(("parallel", "parallel")), name="na_attn",
    )(qkv, qkv, qkv, qkv, qkv, qkv, qkv, bias_tab)


def _na_bias_table(rpb):
    qc = np.arange(GRID_W)[:, None]
    kc = np.arange(GRID_W)[None, :]
    win_start = np.clip(qc - NA_COLS // 2, 0, GRID_W - NA_COLS)
    valid = (kc >= win_start) & (kc < win_start + NA_COLS)
    dc = np.clip(kc - qc + NA_COLS - 1, 0, 2 * NA_COLS - 2)
    tab = jnp.where(jnp.asarray(valid)[None, None], rpb[:, :, dc], MASK_VALUE)
    dr = np.arange(NA_ROWS)[:, None] + np.arange(NA_ROWS)[None, :]
    t = tab[:, dr]
    t = t.transpose(1, 0, 3, 2, 4)
    return t.reshape(NA_ROWS, NA_HEADS, GRID_W, NA_ROWS * GRID_W).astype(F32)


def _mla_kernel(q_ref, k_ref, vt_ref, o_ref, *, tk):
    q = q_ref[0, 0]
    tq = q.shape[0]
    nk = k_ref.shape[2] // tk

    def body(c, carry):
        m, acc = carry
        koff = pl.multiple_of(c * tk, tk)
        st = _dot_nt(k_ref[0, 0, pl.ds(koff, tk), :], q)
        m_new = jnp.maximum(m, jnp.max(st, axis=0, keepdims=True))
        p = jnp.exp2(st - m_new).astype(BF)
        acc = acc * jnp.exp2(m - m_new) + _dot(vt_ref[0, 0, :, pl.ds(koff, tk)], p)
        return m_new, acc

    m0 = jnp.full((1, tq), MASK_VALUE, F32)
    acc0 = jnp.zeros((VT_ROWS, tq), F32)
    _, acc = lax.fori_loop(0, nk, body, (m0, acc0))
    o_ref[0, 0] = (acc[0:MLA_V] / acc[MLA_V:MLA_V + 1]).astype(BF)


def _mla_call(q, k, vt, tq, tk):
    b, h, s, _ = q.shape
    return pl.pallas_call(
        functools.partial(_mla_kernel, tk=tk),
        grid=(b, h, s // tq),
        in_specs=[pl.BlockSpec((1, 1, tq, HEAD_PAD), lambda i, j, t: (i, j, t, 0)),
                  pl.BlockSpec((1, 1, s, HEAD_PAD), lambda i, j, t: (i, j, 0, 0)),
                  pl.BlockSpec((1, 1, VT_ROWS, s), lambda i, j, t: (i, j, 0, 0))],
        out_specs=pl.BlockSpec((1, 1, MLA_V, tq), lambda i, j, t: (i, j, 0, t)),
        out_shape=jax.ShapeDtypeStruct((b, h, MLA_V, s), BF),
        compiler_params=_params(("parallel", "parallel", "parallel")), name="mla_attn",
    )(q, k, vt)


def _merge_kernel(x_ref, ona_ref, omt_ref, gate_ref, wna_ref, wmla_ref, wout_ref, g_ref, b_ref,
                  o_ref, ob_ref, *, alpha):
    d = x_ref.shape[2]
    a = _dot(ona_ref[0], wna_ref[...])
    m = _dot_tn(omt_ref[0], wmla_ref[...])
    gate = gate_ref[0]
    merged = gate[:, :d].astype(F32) * a + gate[:, d:].astype(F32) * m
    y = alpha * x_ref[0] + _dot(merged.astype(BF), wout_ref[...])
    out = _layer_norm(y, g_ref[...], b_ref[...])
    o_ref[0] = out
    ob_ref[0] = out.astype(BF)


def _merge_call(x, o_na, o_mla_t, gates, w, alpha, tm):
    b, s, d = x.shape
    weights = [w['wnao'], w['wmlao'], w['wout'], w['ln1g'], w['ln1b']]
    in_specs = [pl.BlockSpec((1, tm, d), lambda i, j: (i, j, 0)),
                pl.BlockSpec((1, tm, NA_WIDTH), lambda i, j: (i, j, 0)),
                pl.BlockSpec((1, MLA_HEADS * MLA_V, tm), lambda i, j: (i, 0, j)),
                pl.BlockSpec((1, tm, 2 * d), lambda i, j: (i, j, 0))] + [_const_spec(a.shape) for a in weights]
    return pl.pallas_call(
        functools.partial(_merge_kernel, alpha=alpha),
        grid=(b, s // tm), in_specs=in_specs,
        out_specs=(pl.BlockSpec((1, tm, d), lambda i, j: (i, j, 0)),
                   pl.BlockSpec((1, tm, d), lambda i, j: (i, j, 0))),
        out_shape=(jax.ShapeDtypeStruct((b, s, d), F32), jax.ShapeDtypeStruct((b, s, d), BF)),
        compiler_params=_params(("parallel", "parallel")), name="merge_ln1",
    )(x, o_na, o_mla_t, gates, *weights)


def _ple_ln2(x, xb, f, p_ref, wpg_ref, wp_ref, g_ref, b_ref, alpha):
    ple = jax.nn.sigmoid(_dot(xb, wpg_ref[...])) * _dot(p_ref[0].astype(BF), wp_ref[...])
    return _layer_norm(alpha * x + f + ple, g_ref[...], b_ref[...])


def _ffn_kernel(x_ref, p_ref, w1_ref, w3_ref, w2_ref, wpg_ref, wp_ref, g_ref, b_ref, o_ref,
                *, alpha, chunks):
    x = x_ref[0]
    xb = x.astype(BF)
    fc = w1_ref.shape[1] // chunks
    f = None
    for c in range(chunks):
        sl = slice(c * fc, (c + 1) * fc)
        h = _silu(_dot(xb, w1_ref[:, sl])) * _dot(xb, w3_ref[:, sl])
        part = _dot(h.astype(BF), w2_ref[sl, :])
        f = part if f is None else f + part
    o_ref[0] = _ple_ln2(x, xb, f, p_ref, wpg_ref, wp_ref, g_ref, b_ref, alpha)


def _ffn_call(x, p, w, alpha, tm, chunks):
    b, s, d = x.shape
    weights = [w['w1'], w['w3'], w['w2'], w['wpg'], w['wp'], w['ln2g'], w['ln2b']]
    in_specs = [pl.BlockSpec((1, tm, d), lambda i, j: (i, j, 0)),
                pl.BlockSpec((1, tm, p.shape[-1]), lambda i, j: (i, j, 0))] + [_const_spec(a.shape) for a in weights]
    return pl.pallas_call(
        functools.partial(_ffn_kernel, alpha=alpha, chunks=chunks),
        grid=(b, s // tm), in_specs=in_specs,
        out_specs=pl.BlockSpec((1, tm, d), lambda i, j: (i, j, 0)),
        out_shape=jax.ShapeDtypeStruct((b, s, d), F32),
        compiler_params=_params(("parallel", "parallel")), name="ffn_ple_ln2",
    )(x, p, *weights)


def _router_kernel(x_ref, wh_ref, wl_ref, b_ref, o_ref):
    x = x_ref[...]
    xh = x.astype(BF)
    xl = (x - xh.astype(F32)).astype(BF)
    lg = _dot_nt(wh_ref[...], xh) + _dot_nt(wh_ref[...], xl) + _dot_nt(wl_ref[...], xh) + b_ref[...]
    rid = lax.broadcasted_iota(jnp.int32, lg.shape, 0)
    pad = lg.shape[0]
    m1 = jnp.max(lg, axis=0, keepdims=True)
    i1 = jnp.min(jnp.where(lg == m1, rid, pad), axis=0, keepdims=True)
    lg2 = jnp.where(rid == i1, MASK_VALUE, lg)
    m2 = jnp.max(lg2, axis=0, keepdims=True)
    i2 = jnp.min(jnp.where(lg2 == m2, rid, pad), axis=0, keepdims=True)
    e = jnp.exp(m2 - m1)
    o_ref[0:1, :] = i1.astype(F32)
    o_ref[1:2, :] = i2.astype(F32)
    o_ref[2:3, :] = 1.0 / (1.0 + e)
    o_ref[3:4, :] = e / (1.0 + e)
    o_ref[4:8, :] = jnp.zeros((4, lg.shape[1]), F32)


def _router_call(x2, wh, wl, bias, tm):
    n, d = x2.shape
    return pl.pallas_call(
        _router_kernel, grid=(n // tm,),
        in_specs=[pl.BlockSpec((tm, d), lambda i: (i, 0)), _const_spec(wh.shape), _const_spec(wl.shape),
                  _const_spec(bias.shape)],
        out_specs=pl.BlockSpec((8, tm), lambda i: (0, i)),
        out_shape=jax.ShapeDtypeStruct((8, n), F32),
        compiler_params=_params(("parallel",)), name="router",
    )(x2, wh, wl, bias)


def _experts_kernel(te_ref, na_ref, xs_ref, w1_ref, w3_ref, w2_ref, o_ref, acc_ref, *, chunks):
    t = pl.program_id(0)
    c = pl.program_id(1)
    active = t < na_ref[0]

    @pl.when(active)
    def _():
        xb = xs_ref[...]
        h = _silu(_dot(xb, w1_ref[0])) * _dot(xb, w3_ref[0])
        part = _dot(h.astype(BF), w2_ref[0])

        @pl.when(c == 0)
        def _():
            acc_ref[...] = part

        @pl.when(c > 0)
        def _():
            acc_ref[...] += part

        @pl.when(c == chunks - 1)
        def _():
            o_ref[...] = acc_ref[...].astype(BF)

    @pl.when(jnp.logical_and(jnp.logical_not(active), c == chunks - 1))
    def _():
        o_ref[...] = jnp.zeros_like(o_ref)


def _experts_call(tile_expert, n_active, xs, w1, w3, w2, tm, chunks):
    p_rows, d = xs.shape
    fc = w1.shape[2] // chunks
    grid_spec = pltpu.PrefetchScalarGridSpec(
        num_scalar_prefetch=2, grid=(p_rows // tm, chunks),
        in_specs=[pl.BlockSpec((tm, d), lambda t, c, te, na: (t, 0)),
                  pl.BlockSpec((1, d, fc), lambda t, c, te, na: (te[t], 0, c)),
                  pl.BlockSpec((1, d, fc), lambda t, c, te, na: (te[t], 0, c)),
                  pl.BlockSpec((1, fc, d), lambda t, c, te, na: (te[t], c, 0))],
        out_specs=pl.BlockSpec((tm, d), lambda t, c, te, na: (t, 0)),
        scratch_shapes=[pltpu.VMEM((tm, d), F32)])
    return pl.pallas_call(
        functools.partial(_experts_kernel, chunks=chunks), grid_spec=grid_spec,
        out_shape=jax.ShapeDtypeStruct((p_rows, d), BF),
        compiler_params=_params(("arbitrary", "arbitrary")), name="experts",
    )(tile_expert, n_active, xs, w1, w3, w2)


def _combine_kernel(x_ref, ya_ref, yb_ref, gw_ref, p_ref, wpg_ref, wp_ref, g_ref, b_ref, o_ref, *, alpha):
    x = x_ref[0]
    gw = gw_ref[0]
    f = gw[:, 0:1] * ya_ref[0].astype(F32) + gw[:, 1:2] * yb_ref[0].astype(F32)
    o_ref[0] = _ple_ln2(x, x.astype(BF), f, p_ref, wpg_ref, wp_ref, g_ref, b_ref, alpha)


def _combine_call(x, ya, yb, gw, p, w, alpha, tm):
    b, s, d = x.shape
    weights = [w['wpg'], w['wp'], w['ln2g'], w['ln2b']]
    tok = lambda width: pl.BlockSpec((1, tm, width), lambda i, j: (i, j, 0))
    in_specs = [tok(d), tok(d), tok(d), tok(2), tok(p.shape[-1])] + [_const_spec(a.shape) for a in weights]
    return pl.pallas_call(
        functools.partial(_combine_kernel, alpha=alpha),
        grid=(b, s // tm), in_specs=in_specs, out_specs=tok(d),
        out_shape=jax.ShapeDtypeStruct((b, s, d), F32),
        compiler_params=_params(("parallel", "parallel")), name="moe_combine_ple_ln2",
    )(x, ya, yb, gw, p, *weights)


def _moe_layer(x1, x1b, p, w, alpha, tl):
    b, s, d = x1.shape
    n = b * s
    tm = tl['moe_tm']
    route = _router_call(x1.reshape(n, d), w['wrh'], w['wrl'], w['br'], tl['tm'])
    e_all = route[0:2].astype(jnp.int32).reshape(2 * n)
    gw = route[2:4].T.reshape(b, s, 2)

    onehot = (e_all[:, None] == jnp.arange(N_EXPERTS, dtype=jnp.int32)[None, :]).astype(jnp.int32)
    csum = jnp.cumsum(onehot, axis=0)
    rank = jnp.sum(onehot * (csum - 1), axis=1)
    counts = csum[-1]
    padded = ((counts + tm - 1) // tm) * tm
    ends = jnp.cumsum(padded)
    starts = ends - padded
    dest = starts[e_all] + rank
    p_rows = 2 * n + N_EXPERTS * tm
    tok = jnp.tile(jnp.arange(n, dtype=jnp.int32), 2)
    row_token = jnp.zeros((p_rows,), jnp.int32).at[dest].set(tok)
    n_active = (ends[-1] // tm).astype(jnp.int32).reshape(1)
    tile_start = jnp.minimum(jnp.arange(p_rows // tm, dtype=jnp.int32), n_active[0] - 1) * tm
    tile_expert = jnp.sum((tile_start[:, None] >= ends[None, :]).astype(jnp.int32), axis=1)
    tile_expert = jnp.minimum(tile_expert, N_EXPERTS - 1).astype(jnp.int32)

    xs = jnp.take(x1b.reshape(n, d), row_token, axis=0)
    ys = _experts_call(tile_expert, n_active, xs, w['mw1'], w['mw3'], w['mw2'], tm, tl['moe_chunks'])
    ya = jnp.take(ys, dest[:n], axis=0).reshape(b, s, d)
    yb = jnp.take(ys, dest[n:], axis=0).reshape(b, s, d)
    return _combine_call(x1, ya, yb, gw, p, w, alpha, tl['tm'])


def _rope_tables(s):
    pos = jnp.arange(s, dtype=F32)
    inv_freq = ROPE_THETA ** (-jnp.arange(0, MLA_ROPE // 2, dtype=F32) * (2.0 / MLA_ROPE))
    ang = pos[:, None] * inv_freq[None, :]
    cos, sin = jnp.cos(ang), jnp.sin(ang)
    half = MLA_ROPE // 2
    ck = jnp.concatenate([jnp.ones((s, MLA_NOPE), F32), cos, cos,
                          jnp.zeros((s, HEAD_PAD - MLA_NOPE - MLA_ROPE), F32)], axis=1)
    sk = jnp.concatenate([jnp.zeros((s, MLA_NOPE), F32), sin, sin,
                          jnp.zeros((s, HEAD_PAD - MLA_NOPE - MLA_ROPE), F32)], axis=1)
    del half
    qs = (MLA_NOPE + MLA_ROPE) ** -0.5 * math.log2(math.e)
    return ck * qs, sk * qs, ck, sk


def _rotate_half_cols(w_rope):
    half = w_rope.shape[-1] // 2
    return jnp.concatenate([-w_rope[..., half:], w_rope[..., :half]], axis=-1)


def _prep_layer(i, w_in, b_gate, q_norm_g, w_uq, kv_norm_g, w_ukv, w_na_o, w_mla_o, w_out, ln1_g, ln1_b,
                w_ple_gate, w_ple, ln2_g, ln2_b):
    d = w_in.shape[1]
    o = 0
    wna = w_in[i, :, o:o + 3 * NA_WIDTH]; o += 3 * NA_WIDTH
    wql = w_in[i, :, o:o + MLA_Q_LORA]; o += MLA_Q_LORA
    wkvl = w_in[i, :, o:o + MLA_KV_LORA]; o += MLA_KV_LORA
    wkr = w_in[i, :, o:o + MLA_ROPE]; o += MLA_ROPE
    wg = w_in[i, :, o:o + 2 * d]
    na_scale = jnp.concatenate([jnp.full((NA_WIDTH,), NA_HEAD_DIM ** -0.5, F32), jnp.ones((2 * NA_WIDTH,), F32)])
    wna = wna * na_scale[None, :]

    zpad = HEAD_PAD - MLA_NOPE - MLA_ROPE
    uq = w_uq[i].reshape(MLA_Q_LORA, MLA_HEADS, MLA_NOPE + MLA_ROPE)
    uq_pad = jnp.pad(uq, ((0, 0), (0, 0), (0, zpad)))
    uq_rot = jnp.pad(_rotate_half_cols(uq[..., MLA_NOPE:]), ((0, 0), (0, 0), (MLA_NOPE, zpad)))
    kr_pad = jnp.pad(wkr, ((0, 0), (MLA_NOPE, zpad)))
    kr_rot = jnp.pad(_rotate_half_cols(wkr), ((0, 0), (MLA_NOPE, zpad)))
    ukv = w_ukv[i].reshape(MLA_KV_LORA, MLA_HEADS, MLA_NOPE + MLA_V)
    uk_pad = jnp.pad(ukv[..., :MLA_NOPE], ((0, 0), (0, 0), (0, HEAD_PAD - MLA_NOPE)))
    uvt = ukv[..., MLA_NOPE:].reshape(MLA_KV_LORA, MLA_HEADS * MLA_V).T

    return dict(
        wna=wna.astype(BF), wg=wg.astype(BF), bg=b_gate[i][None, :], wql=wql.astype(BF), wkvl=wkvl.astype(BF),
        wkr=jnp.concatenate([kr_pad, kr_rot], axis=1).astype(BF),
        gq=q_norm_g[i][None, :], gkv=kv_norm_g[i][None, :],
        wuq=uq_pad.reshape(MLA_Q_LORA, MLA_HEADS * HEAD_PAD).astype(BF),
        wuqr=uq_rot.reshape(MLA_Q_LORA, MLA_HEADS * HEAD_PAD).astype(BF),
        wuk=uk_pad.reshape(MLA_KV_LORA, MLA_HEADS * HEAD_PAD).astype(BF), wuvt=uvt.astype(BF),
        wnao=w_na_o[i].astype(BF), wmlao=w_mla_o[i].astype(BF), wout=w_out[i].astype(BF),
        ln1g=ln1_g[i][None, :], ln1b=ln1_b[i][None, :],
        wpg=w_ple_gate[i].astype(BF), wp=w_ple[i].astype(BF), ln2g=ln2_g[i][None, :], ln2b=ln2_b[i][None, :],
    )


def kernel(x, p, w_in, b_gate, q_norm_g, w_uq, kv_norm_g, w_ukv, na_rpb, w_na_o, w_mla_o, w_out, ln1_g, ln1_b,
           ffn_w1, ffn_w3, ffn_w2, moe_w_router, moe_b_router, moe_w1, moe_w3, moe_w2, w_ple_gate, w_ple,
           ln2_g, ln2_b):
    b, s, d = x.shape
    depth = w_in.shape[0]
    alpha = (2 * depth) ** 0.25
    tl = _tiles(b * s, s)
    tabs = _rope_tables(s)

    for i in range(depth):
        w = _prep_layer(i, w_in, b_gate, q_norm_g, w_uq, kv_norm_g, w_ukv, w_na_o, w_mla_o, w_out, ln1_g, ln1_b,
                        w_ple_gate, w_ple, ln2_g, ln2_b)
        qkv, gates, q_mla, k_mla, vt = _proj_call(x, w, tabs, tl['tm'])
        o_na = _na_call(qkv, _na_bias_table(na_rpb[i]), tl['na_rows'])
        o_mla_t = _mla_call(q_mla, k_mla, vt, tl['tq'], tl['tk'])
        o_mla_t = o_mla_t.reshape(b, MLA_HEADS * MLA_V, s)
        x1, x1b = _merge_call(x, o_na, o_mla_t, gates, w, alpha, tl['tm'])
        j = i // 2
        if i % 2 == 0:
            w.update(w1=ffn_w1[j].astype(BF), w3=ffn_w3[j].astype(BF), w2=ffn_w2[j].astype(BF))
            x = _ffn_call(x1, p[i], w, alpha, tl['tm'], tl['ffn_chunks'])
        else:
            wr = jnp.pad(moe_w_router[j].T, ((0, BF16_SUBLANES - N_EXPERTS), (0, 0)))
            wrh = wr.astype(BF)
            br = jnp.concatenate([moe_b_router[j], jnp.full((BF16_SUBLANES - N_EXPERTS,), MASK_VALUE, F32)])
            w.update(wrh=wrh, wrl=(wr - wrh.astype(F32)).astype(BF), br=br[:, None],
                     mw1=moe_w1[j].astype(BF), mw3=moe_w3[j].astype(BF), mw2=moe_w2[j].astype(BF))
            x = _moe_layer(x1, x1b, p[i], w, alpha, tl)
    return x
```

```python
import functools
import math

import numpy as np
import jax
import jax.numpy as jnp
from jax import lax
from jax.experimental import pallas as pl
from jax.experimental.pallas import tpu as pltpu

BF = jnp.bfloat16
F32 = jnp.float32

GRID_W = 64
NA_HEADS = 8
NA_HEAD_DIM = 64
NA_ROWS = 8
NA_COLS = 16
NA_WIDTH = NA_HEADS * NA_HEAD_DIM
MLA_HEADS = 8
MLA_NOPE = 64
MLA_ROPE = 32
MLA_V = 64
MLA_Q_LORA = 768
MLA_KV_LORA = 256
ROPE_THETA = 10000.0
N_EXPERTS = 8
LN_EPS = 1e-5
RMS_EPS = 1e-6
MASK_VALUE = -1e30

LANES = 128
BF16_SUBLANES = 16
MXU_COLS = 256
VMEM_LIMIT_BYTES = 56 * 1024 * 1024

HEAD_PAD = LANES
VT_ROWS = MLA_V + BF16_SUBLANES


def _dot(a, b):
    return jnp.dot(a, b, preferred_element_type=F32)


def _dot_nt(a, b):
    return lax.dot_general(a, b, (((1,), (1,)), ((), ())), preferred_element_type=F32)


def _dot_tn(a, b):
    return lax.dot_general(a, b, (((0,), (0,)), ((), ())), preferred_element_type=F32)


def _const_spec(shape):
    nd = len(shape)
    return pl.BlockSpec(shape, lambda *_: (0,) * nd, pipeline_mode=pl.Buffered(1))


def _params(sem):
    return pltpu.CompilerParams(dimension_semantics=sem, vmem_limit_bytes=VMEM_LIMIT_BYTES)


def _tiles(n_tokens, seq):
    return dict(
        tm=min(512, seq),
        tq=min(1024, seq),
        tk=min(256, seq),
        mla_unroll=8 if (seq // min(256, seq)) % 8 == 0 else 2,
        na_rows=8,
        moe_tm=1024 if n_tokens >= 8192 else 128,
        moe_chunks=4,
        ffn_chunks=2,
    )


def _layer_norm(y, g, b):
    mu = jnp.mean(y, axis=-1, keepdims=True)
    yc = y - mu
    var = jnp.mean(yc * yc, axis=-1, keepdims=True)
    return yc * lax.rsqrt(var + LN_EPS) * g + b


def _rms_norm(y, g):
    return y * lax.rsqrt(jnp.mean(y * y, axis=-1, keepdims=True) + RMS_EPS) * g


def _silu(y):
    return y * jax.nn.sigmoid(y)


def _proj_kernel(x_ref, wna_ref, wg_ref, bg_ref, wql_ref, wkvl_ref, wkr_ref, gq_ref, gkv_ref,
                 wuq_ref, wuqr_ref, wuk_ref, wuvt_ref, cq_ref, sq_ref, ck_ref, sk_ref,
                 qkv_ref, gate_ref, q_ref, k_ref, vt_ref):
    xb = x_ref[0].astype(BF)
    qkv_ref[0] = _dot(xb, wna_ref[...]).astype(BF)
    gate_ref[0] = jax.nn.sigmoid(_dot(xb, wg_ref[...]) + bg_ref[...]).astype(BF)

    qn = _rms_norm(_dot(xb, wql_ref[...]), gq_ref[...]).astype(BF)
    qa = _dot(qn, wuq_ref[...])
    qr = _dot(qn, wuqr_ref[...])
    cq, sq = cq_ref[...], sq_ref[...]
    for h in range(MLA_HEADS):
        sl = slice(h * HEAD_PAD, (h + 1) * HEAD_PAD)
        q_ref[0, h] = (qa[:, sl] * cq + qr[:, sl] * sq).astype(BF)

    kvn = _rms_norm(_dot(xb, wkvl_ref[...]), gkv_ref[...]).astype(BF)
    kr2 = _dot(xb, wkr_ref[...])
    kr = kr2[:, :HEAD_PAD] * ck_ref[...] + kr2[:, HEAD_PAD:] * sk_ref[...]
    kn = _dot(kvn, wuk_ref[...])
    for h in range(MLA_HEADS):
        sl = slice(h * HEAD_PAD, (h + 1) * HEAD_PAD)
        k_ref[0, h] = (kn[:, sl] + kr).astype(BF)

    vt = _dot_nt(wuvt_ref[...], kvn)
    ones = jnp.ones((BF16_SUBLANES, vt.shape[1]), BF)
    for h in range(MLA_HEADS):
        vt_ref[0, h, 0:MLA_V, :] = vt[h * MLA_V:(h + 1) * MLA_V].astype(BF)
        vt_ref[0, h, MLA_V:VT_ROWS, :] = ones


def _proj_call(x, w, tabs, tm):
    b, s, d = x.shape
    grid = (b, s // tm)
    weights = [w['wna'], w['wg'], w['bg'], w['wql'], w['wkvl'], w['wkr'], w['gq'], w['gkv'],
               w['wuq'], w['wuqr'], w['wuk'], w['wuvt']]
    tab_spec = pl.BlockSpec((tm, HEAD_PAD), lambda i, j: (j, 0))
    in_specs = ([pl.BlockSpec((1, tm, d), lambda i, j: (i, j, 0))]
                + [_const_spec(a.shape) for a in weights] + [tab_spec] * 4)
    out_shape = (
        jax.ShapeDtypeStruct((b, s, 3 * NA_WIDTH), BF),
        jax.ShapeDtypeStruct((b, s, 2 * d), BF),
        jax.ShapeDtypeStruct((b, MLA_HEADS, s, HEAD_PAD), BF),
        jax.ShapeDtypeStruct((b, MLA_HEADS, s, HEAD_PAD), BF),
        jax.ShapeDtypeStruct((b, MLA_HEADS, VT_ROWS, s), BF),
    )
    out_specs = (
        pl.BlockSpec((1, tm, 3 * NA_WIDTH), lambda i, j: (i, j, 0)),
        pl.BlockSpec((1, tm, 2 * d), lambda i, j: (i, j, 0)),
        pl.BlockSpec((1, MLA_HEADS, tm, HEAD_PAD), lambda i, j: (i, 0, j, 0)),
        pl.BlockSpec((1, MLA_HEADS, tm, HEAD_PAD), lambda i, j: (i, 0, j, 0)),
        pl.BlockSpec((1, MLA_HEADS, VT_ROWS, tm), lambda i, j: (i, 0, 0, j)),
    )
    return pl.pallas_call(
        _proj_kernel, grid=grid, in_specs=in_specs, out_specs=out_specs, out_shape=out_shape,
        compiler_params=_params(("parallel", "parallel")), name="proj",
    )(x, *weights, *tabs)


def _na_kernel(q_ref, kp_ref, kc_ref, kn_ref, vp_ref, vc_ref, vn_ref, bias_ref, o_ref,
               kcat_ref, vcat_ref, *, rows, rb):
    i = pl.program_id(1)
    blk = rb * GRID_W
    win = NA_ROWS * GRID_W
    kcat_ref[0:blk] = kp_ref[0]
    kcat_ref[blk:2 * blk] = kc_ref[0]
    kcat_ref[2 * blk:3 * blk] = kn_ref[0]
    vcat_ref[0:blk] = vp_ref[0]
    vcat_ref[blk:2 * blk] = vc_ref[0]
    vcat_ref[2 * blk:3 * blk] = vn_ref[0]
    lane = lax.broadcasted_iota(jnp.int32, (GRID_W, LANES), 1)
    first_head = lane < NA_HEAD_DIM

    def row(a, carry):
        r = i * rb + a
        rs = jnp.clip(r - NA_ROWS // 2, 0, rows - NA_ROWS)
        off = pl.multiple_of((rs - (i - 1) * rb) * GRID_W, GRID_W)
        d = rs - r + (NA_ROWS - 1)
        qoff = pl.multiple_of(a * GRID_W, GRID_W)
        qrow = q_ref[0, pl.ds(qoff, GRID_W), :]
        zero = jnp.zeros((GRID_W, LANES), BF)
        sc = []
        for hp in range(NA_HEADS // 2):
            sl = slice(hp * LANES, (hp + 1) * LANES)
            qp = qrow[:, sl]
            qm = jnp.concatenate([jnp.where(first_head, qp, zero), jnp.where(first_head, zero, qp)], axis=0)
            sc.append(_dot_nt(qm, kcat_ref[pl.ds(off, win), sl]))
        sc = jnp.concatenate(sc, axis=0) + bias_ref[d]
        m = jnp.max(sc, axis=-1, keepdims=True)
        p = jnp.exp(sc - m)
        inv_l = 1.0 / jnp.sum(p, axis=-1, keepdims=True)
        pb = p.astype(BF)
        for hp in range(NA_HEADS // 2):
            sl = slice(hp * LANES, (hp + 1) * LANES)
            rows2 = slice(hp * 2 * GRID_W, (hp + 1) * 2 * GRID_W)
            o2 = _dot(pb[rows2], vcat_ref[pl.ds(off, win), sl]) * inv_l[rows2]
            o_ref[0, pl.ds(qoff, GRID_W), sl] = jnp.where(first_head, o2[:GRID_W], o2[GRID_W:]).astype(BF)
        return carry

    lax.fori_loop(0, rb, row, 0, unroll=2)


def _na_call(qkv, bias_tab, rb):
    b, s, _ = qkv.shape
    rows = s // GRID_W
    nb = rows // rb
    blk = rb * GRID_W

    def kv_spec(col, shift):
        return pl.BlockSpec((1, blk, NA_WIDTH),
                            lambda i, j: (i, jnp.clip(j + shift, 0, nb - 1), col))

    in_specs = [pl.BlockSpec((1, blk, NA_WIDTH), lambda i, j: (i, j, 0)),
                kv_spec(1, -1), kv_spec(1, 0), kv_spec(1, 1),
                kv_spec(2, -1), kv_spec(2, 0), kv_spec(2, 1),
                _const_spec(bias_tab.shape)]
    return pl.pallas_call(
        functools.partial(_na_kernel, rows=rows, rb=rb),
        grid=(b, nb), in_specs=in_specs,
        out_specs=pl.BlockSpec((1, blk, NA_WIDTH), lambda i, j: (i, j, 0)),
        out_shape=jax.ShapeDtypeStruct((b, s, NA_WIDTH), BF),
        scratch_shapes=[pltpu.VMEM((3 * blk, NA_WIDTH), BF), pltpu.VMEM((3 * blk, NA_WIDTH), BF)],
        compiler_params=_params(("parallel", "parallel")), name="na_attn",
    )(qkv, qkv, qkv, qkv, qkv, qkv, qkv, bias_tab)


def _na_bias_table(rpb):
    qc = np.arange(GRID_W)[:, None]
    kc = np.arange(GRID_W)[None, :]
    win_start = np.clip(qc - NA_COLS // 2, 0, GRID_W - NA_COLS)
    valid = (kc >= win_start) & (kc < win_start + NA_COLS)
    dc = np.clip(kc - qc + NA_COLS - 1, 0, 2 * NA_COLS - 2)
    tab = jnp.where(jnp.asarray(valid)[None, None], rpb[:, :, dc], MASK_VALUE)
    dr = np.arange(NA_ROWS)[:, None] + np.arange(NA_ROWS)[None, :]
    t = tab[:, dr]
    t = t.transpose(1, 0, 3, 2, 4)
    return t.reshape(NA_ROWS, NA_HEADS * GRID_W, NA_ROWS * GRID_W).astype(F32)


def _mla_kernel(q_ref, k_ref, vt_ref, o_ref, s0_ref, s1_ref, acc_ref, m_ref, *, tk, unroll):
    q = q_ref[0, 0]
    nk = k_ref.shape[2] // tk
    s_refs = (s0_ref, s1_ref)

    def key_slice(c):
        return pl.ds(pl.multiple_of(c * tk, tk), tk)

    def scores(c, slot):
        s_refs[slot][...] = _dot_nt(k_ref[0, 0, key_slice(c), :], q)

    def consume(c, slot):
        s_ref = s_refs[slot]
        vt = vt_ref[0, 0, :, key_slice(c)]
        for j in range(q.shape[0] // MXU_COLS):
            cs = slice(j * MXU_COLS, (j + 1) * MXU_COLS)
            m = m_ref[:, cs]
            m_new = jnp.maximum(m, jnp.max(s_ref[:, cs], axis=0, keepdims=True))
            p = jnp.exp2(s_ref[:, cs] - m_new).astype(BF)
            acc_ref[:, cs] = acc_ref[:, cs] * jnp.exp2(m - m_new) + _dot(vt, p)
            m_ref[:, cs] = m_new

    scores(0, 0)
    m_ref[...] = jnp.full(m_ref.shape, MASK_VALUE, F32)
    acc_ref[...] = jnp.zeros(acc_ref.shape, F32)

    def body(i, carry):
        c0 = unroll * i
        for u in range(unroll):
            scores(c0 + u + 1, (u + 1) % 2)
            consume(c0 + u, u % 2)
        return carry

    lax.fori_loop(0, nk // unroll - 1, body, 0)
    c0 = nk - unroll
    for u in range(unroll):
        if u + 1 < unroll:
            scores(c0 + u + 1, (u + 1) % 2)
        consume(c0 + u, u % 2)
    acc = acc_ref[...]
    o_ref[0, 0] = (acc[0:MLA_V] / acc[MLA_V:MLA_V + 1]).astype(BF)


def _mla_call(q, k, vt, tq, tk, unroll):
    b, h, s, _ = q.shape
    assert unroll % 2 == 0 and (s // tk) % unroll == 0
    return pl.pallas_call(
        functools.partial(_mla_kernel, tk=tk, unroll=unroll),
        grid=(b, h, s // tq),
        in_specs=[pl.BlockSpec((1, 1, tq, HEAD_PAD), lambda i, j, t: (i, j, t, 0)),
                  pl.BlockSpec((1, 1, s, HEAD_PAD), lambda i, j, t: (i, j, 0, 0)),
                  pl.BlockSpec((1, 1, VT_ROWS, s), lambda i, j, t: (i, j, 0, 0))],
        out_specs=pl.BlockSpec((1, 1, MLA_V, tq), lambda i, j, t: (i, j, 0, t)),
        out_shape=jax.ShapeDtypeStruct((b, h, MLA_V, s), BF),
        scratch_shapes=[pltpu.VMEM((tk, tq), F32), pltpu.VMEM((tk, tq), F32),
                        pltpu.VMEM((VT_ROWS, tq), F32), pltpu.VMEM((1, tq), F32)],
        compiler_params=_params(("parallel", "parallel", "parallel")), name="mla_attn",
    )(q, k, vt)


def _merge_kernel(x_ref, ona_ref, omt_ref, gate_ref, wna_ref, wmla_ref, wout_ref, g_ref, b_ref,
                  o_ref, ob_ref, *, alpha):
    d = x_ref.shape[2]
    a = _dot(ona_ref[0], wna_ref[...])
    m = _dot_tn(omt_ref[0], wmla_ref[...])
    gate = gate_ref[0]
    merged = gate[:, :d].astype(F32) * a + gate[:, d:].astype(F32) * m
    y = alpha * x_ref[0] + _dot(merged.astype(BF), wout_ref[...])
    out = _layer_norm(y, g_ref[...], b_ref[...])
    o_ref[0] = out
    ob_ref[0] = out.astype(BF)


def _merge_call(x, o_na, o_mla_t, gates, w, alpha, tm):
    b, s, d = x.shape
    weights = [w['wnao'], w['wmlao'], w['wout'], w['ln1g'], w['ln1b']]
    in_specs = [pl.BlockSpec((1, tm, d), lambda i, j: (i, j, 0)),
                pl.BlockSpec((1, tm, NA_WIDTH), lambda i, j: (i, j, 0)),
                pl.BlockSpec((1, MLA_HEADS * MLA_V, tm), lambda i, j: (i, 0, j)),
                pl.BlockSpec((1, tm, 2 * d), lambda i, j: (i, j, 0))] + [_const_spec(a.shape) for a in weights]
    return pl.pallas_call(
        functools.partial(_merge_kernel, alpha=alpha),
        grid=(b, s // tm), in_specs=in_specs,
        out_specs=(pl.BlockSpec((1, tm, d), lambda i, j: (i, j, 0)),
                   pl.BlockSpec((1, tm, d), lambda i, j: (i, j, 0))),
        out_shape=(jax.ShapeDtypeStruct((b, s, d), F32), jax.ShapeDtypeStruct((b, s, d), BF)),
        compiler_params=_params(("parallel", "parallel")), name="merge_ln1",
    )(x, o_na, o_mla_t, gates, *weights)


def _ple_ln2(x, xb, f, p_ref, wpg_ref, wp_ref, g_ref, b_ref, alpha):
    ple = jax.nn.sigmoid(_dot(xb, wpg_ref[...])) * _dot(p_ref[0].astype(BF), wp_ref[...])
    return _layer_norm(alpha * x + f + ple, g_ref[...], b_ref[...])


def _ffn_kernel(x_ref, p_ref, w1_ref, w3_ref, w2_ref, wpg_ref, wp_ref, g_ref, b_ref, o_ref,
                *, alpha, chunks):
    x = x_ref[0]
    xb = x.astype(BF)
    fc = w1_ref.shape[1] // chunks
    f = None
    for c in range(chunks):
        sl = slice(c * fc, (c + 1) * fc)
        h = _silu(_dot(xb, w1_ref[:, sl])) * _dot(xb, w3_ref[:, sl])
        part = _dot(h.astype(BF), w2_ref[sl, :])
        f = part if f is None else f + part
    o_ref[0] = _ple_ln2(x, xb, f, p_ref, wpg_ref, wp_ref, g_ref, b_ref, alpha)


def _ffn_call(x, p, w, alpha, tm, chunks):
    b, s, d = x.shape
    weights = [w['w1'], w['w3'], w['w2'], w['wpg'], w['wp'], w['ln2g'], w['ln2b']]
    in_specs = [pl.BlockSpec((1, tm, d), lambda i, j: (i, j, 0)),
                pl.BlockSpec((1, tm, p.shape[-1]), lambda i, j: (i, j, 0))] + [_const_spec(a.shape) for a in weights]
    return pl.pallas_call(
        functools.partial(_ffn_kernel, alpha=alpha, chunks=chunks),
        grid=(b, s // tm), in_specs=in_specs,
        out_specs=pl.BlockSpec((1, tm, d), lambda i, j: (i, j, 0)),
        out_shape=jax.ShapeDtypeStruct((b, s, d), F32),
        compiler_params=_params(("parallel", "parallel")), name="ffn_ple_ln2",
    )(x, p, *weights)


def _router_kernel(x_ref, wh_ref, wl_ref, b_ref, o_ref):
    x = x_ref[...]
    xh = x.astype(BF)
    xl = (x - xh.astype(F32)).astype(BF)
    lg = _dot_nt(wh_ref[...], xh) + _dot_nt(wh_ref[...], xl) + _dot_nt(wl_ref[...], xh) + b_ref[...]
    rid = lax.broadcasted_iota(jnp.int32, lg.shape, 0)
    pad = lg.shape[0]
    m1 = jnp.max(lg, axis=0, keepdims=True)
    i1 = jnp.min(jnp.where(lg == m1, rid, pad), axis=0, keepdims=True)
    lg2 = jnp.where(rid == i1, MASK_VALUE, lg)
    m2 = jnp.max(lg2, axis=0, keepdims=True)
    i2 = jnp.min(jnp.where(lg2 == m2, rid, pad), axis=0, keepdims=True)
    e = jnp.exp(m2 - m1)
    o_ref[0:1, :] = i1.astype(F32)
    o_ref[1:2, :] = i2.astype(F32)
    o_ref[2:3, :] = 1.0 / (1.0 + e)
    o_ref[3:4, :] = e / (1.0 + e)
    o_ref[4:8, :] = jnp.zeros((4, lg.shape[1]), F32)


def _router_call(x2, wh, wl, bias, tm):
    n, d = x2.shape
    return pl.pallas_call(
        _router_kernel, grid=(n // tm,),
        in_specs=[pl.BlockSpec((tm, d), lambda i: (i, 0)), _const_spec(wh.shape), _const_spec(wl.shape),
                  _const_spec(bias.shape)],
        out_specs=pl.BlockSpec((8, tm), lambda i: (0, i)),
        out_shape=jax.ShapeDtypeStruct((8, n), F32),
        compiler_params=_params(("parallel",)), name="router",
    )(x2, wh, wl, bias)


def _experts_kernel(te_ref, na_ref, xs_ref, w1_ref, w3_ref, w2_ref, o_ref, acc_ref, *, chunks):
    t = pl.program_id(0)
    c = pl.program_id(1)
    active = t < na_ref[0]

    @pl.when(active)
    def _():
        xb = xs_ref[...]
        h = _silu(_dot(xb, w1_ref[0])) * _dot(xb, w3_ref[0])
        part = _dot(h.astype(BF), w2_ref[0])

        @pl.when(c == 0)
        def _():
            acc_ref[...] = part

        @pl.when(c > 0)
        def _():
            acc_ref[...] += part

        @pl.when(c == chunks - 1)
        def _():
            o_ref[...] = acc_ref[...].astype(BF)

    @pl.when(jnp.logical_and(jnp.logical_not(active), c == chunks - 1))
    def _():
        o_ref[...] = jnp.zeros_like(o_ref)


def _experts_call(tile_expert, n_active, xs, w1, w3, w2, tm, chunks):
    p_rows, d = xs.shape
    fc = w1.shape[2] // chunks
    grid_spec = pltpu.PrefetchScalarGridSpec(
        num_scalar_prefetch=2, grid=(p_rows // tm, chunks),
        in_specs=[pl.BlockSpec((tm, d), lambda t, c, te, na: (t, 0)),
                  pl.BlockSpec((1, d, fc), lambda t, c, te, na: (te[t], 0, c)),
                  pl.BlockSpec((1, d, fc), lambda t, c, te, na: (te[t], 0, c)),
                  pl.BlockSpec((1, fc, d), lambda t, c, te, na: (te[t], c, 0))],
        out_specs=pl.BlockSpec((tm, d), lambda t, c, te, na: (t, 0)),
        scratch_shapes=[pltpu.VMEM((tm, d), F32)])
    return pl.pallas_call(
        functools.partial(_experts_kernel, chunks=chunks), grid_spec=grid_spec,
        out_shape=jax.ShapeDtypeStruct((p_rows, d), BF),
        compiler_params=_params(("arbitrary", "arbitrary")), name="experts",
    )(tile_expert, n_active, xs, w1, w3, w2)


def _combine_kernel(x_ref, ya_ref, yb_ref, gw_ref, p_ref, wpg_ref, wp_ref, g_ref, b_ref, o_ref, *, alpha):
    x = x_ref[0]
    gw = gw_ref[0]
    f = gw[:, 0:1] * ya_ref[0].astype(F32) + gw[:, 1:2] * yb_ref[0].astype(F32)
    o_ref[0] = _ple_ln2(x, x.astype(BF), f, p_ref, wpg_ref, wp_ref, g_ref, b_ref, alpha)


def _combine_call(x, ya, yb, gw, p, w, alpha, tm):
    b, s, d = x.shape
    weights = [w['wpg'], w['wp'], w['ln2g'], w['ln2b']]
    tok = lambda width: pl.BlockSpec((1, tm, width), lambda i, j: (i, j, 0))
    in_specs = [tok(d), tok(d), tok(d), tok(2), tok(p.shape[-1])] + [_const_spec(a.shape) for a in weights]
    return pl.pallas_call(
        functools.partial(_combine_kernel, alpha=alpha),
        grid=(b, s // tm), in_specs=in_specs, out_specs=tok(d),
        out_shape=jax.ShapeDtypeStruct((b, s, d), F32),
        compiler_params=_params(("parallel", "parallel")), name="moe_combine_ple_ln2",
    )(x, ya, yb, gw, p, *weights)


def _moe_layer(x1, x1b, p, w, alpha, tl):
    b, s, d = x1.shape
    n = b * s
    tm = tl['moe_tm']
    route = _router_call(x1.reshape(n, d), w['wrh'], w['wrl'], w['br'], tl['tm'])
    e_all = route[0:2].astype(jnp.int32).reshape(2 * n)
    gw = route[2:4].T.reshape(b, s, 2)

    onehot = (e_all[:, None] == jnp.arange(N_EXPERTS, dtype=jnp.int32)[None, :]).astype(jnp.int32)
    csum = jnp.cumsum(onehot, axis=0)
    rank = jnp.sum(onehot * (csum - 1), axis=1)
    counts = csum[-1]
    padded = ((counts + tm - 1) // tm) * tm
    ends = jnp.cumsum(padded)
    starts = ends - padded
    dest = starts[e_all] + rank
    p_rows = 2 * n + N_EXPERTS * tm
    tok = jnp.tile(jnp.arange(n, dtype=jnp.int32), 2)
    row_token = jnp.zeros((p_rows,), jnp.int32).at[dest].set(tok)
    n_active = (ends[-1] // tm).astype(jnp.int32).reshape(1)
    tile_start = jnp.minimum(jnp.arange(p_rows // tm, dtype=jnp.int32), n_active[0] - 1) * tm
    tile_expert = jnp.sum((tile_start[:, None] >= ends[None, :]).astype(jnp.int32), axis=1)
    tile_expert = jnp.minimum(tile_expert, N_EXPERTS - 1).astype(jnp.int32)

    xs = jnp.take(x1b.reshape(n, d), row_token, axis=0)
    ys = _experts_call(tile_expert, n_active, xs, w['mw1'], w['mw3'], w['mw2'], tm, tl['moe_chunks'])
    ya = jnp.take(ys, dest[:n], axis=0).reshape(b, s, d)
    yb = jnp.take(ys, dest[n:], axis=0).reshape(b, s, d)
    return _combine_call(x1, ya, yb, gw, p, w, alpha, tl['tm'])


def _rope_tables(s):
    pos = jnp.arange(s, dtype=F32)
    inv_freq = ROPE_THETA ** (-jnp.arange(0, MLA_ROPE // 2, dtype=F32) * (2.0 / MLA_ROPE))
    ang = pos[:, None] * inv_freq[None, :]
    cos, sin = jnp.cos(ang), jnp.sin(ang)
    ck = jnp.concatenate([jnp.ones((s, MLA_NOPE), F32), cos, cos,
                          jnp.zeros((s, HEAD_PAD - MLA_NOPE - MLA_ROPE), F32)], axis=1)
    sk = jnp.concatenate([jnp.zeros((s, MLA_NOPE), F32), sin, sin,
                          jnp.zeros((s, HEAD_PAD - MLA_NOPE - MLA_ROPE), F32)], axis=1)
    qs = (MLA_NOPE + MLA_ROPE) ** -0.5 * math.log2(math.e)
    return ck * qs, sk * qs, ck, sk


def _rotate_half_cols(w_rope):
    half = w_rope.shape[-1] // 2
    return jnp.concatenate([-w_rope[..., half:], w_rope[..., :half]], axis=-1)


def _prep_layer(i, w_in, b_gate, q_norm_g, w_uq, kv_norm_g, w_ukv, w_na_o, w_mla_o, w_out, ln1_g, ln1_b,
                w_ple_gate, w_ple, ln2_g, ln2_b):
    d = w_in.shape[1]
    o = 0
    wna = w_in[i, :, o:o + 3 * NA_WIDTH]; o += 3 * NA_WIDTH
    wql = w_in[i, :, o:o + MLA_Q_LORA]; o += MLA_Q_LORA
    wkvl = w_in[i, :, o:o + MLA_KV_LORA]; o += MLA_KV_LORA
    wkr = w_in[i, :, o:o + MLA_ROPE]; o += MLA_ROPE
    wg = w_in[i, :, o:o + 2 * d]
    na_scale = jnp.concatenate([jnp.full((NA_WIDTH,), NA_HEAD_DIM ** -0.5, F32), jnp.ones((2 * NA_WIDTH,), F32)])
    wna = wna * na_scale[None, :]

    zpad = HEAD_PAD - MLA_NOPE - MLA_ROPE
    uq = w_uq[i].reshape(MLA_Q_LORA, MLA_HEADS, MLA_NOPE + MLA_ROPE)
    uq_pad = jnp.pad(uq, ((0, 0), (0, 0), (0, zpad)))
    uq_rot = jnp.pad(_rotate_half_cols(uq[..., MLA_NOPE:]), ((0, 0), (0, 0), (MLA_NOPE, zpad)))
    kr_pad = jnp.pad(wkr, ((0, 0), (MLA_NOPE, zpad)))
    kr_rot = jnp.pad(_rotate_half_cols(wkr), ((0, 0), (MLA_NOPE, zpad)))
    ukv = w_ukv[i].reshape(MLA_KV_LORA, MLA_HEADS, MLA_NOPE + MLA_V)
    uk_pad = jnp.pad(ukv[..., :MLA_NOPE], ((0, 0), (0, 0), (0, HEAD_PAD - MLA_NOPE)))
    uvt = ukv[..., MLA_NOPE:].reshape(MLA_KV_LORA, MLA_HEADS * MLA_V).T

    return dict(
        wna=wna.astype(BF), wg=wg.astype(BF), bg=b_gate[i][None, :], wql=wql.astype(BF), wkvl=wkvl.astype(BF),
        wkr=jnp.concatenate([kr_pad, kr_rot], axis=1).astype(BF),
        gq=q_norm_g[i][None, :], gkv=kv_norm_g[i][None, :],
        wuq=uq_pad.reshape(MLA_Q_LORA, MLA_HEADS * HEAD_PAD).astype(BF),
        wuqr=uq_rot.reshape(MLA_Q_LORA, MLA_HEADS * HEAD_PAD).astype(BF),
        wuk=uk_pad.reshape(MLA_KV_LORA, MLA_HEADS * HEAD_PAD).astype(BF), wuvt=uvt.astype(BF),
        wnao=w_na_o[i].astype(BF), wmlao=w_mla_o[i].astype(BF), wout=w_out[i].astype(BF),
        ln1g=ln1_g[i][None, :], ln1b=ln1_b[i][None, :],
        wpg=w_ple_gate[i].astype(BF), wp=w_ple[i].astype(BF), ln2g=ln2_g[i][None, :], ln2b=ln2_b[i][None, :],
    )


def kernel(x, p, w_in, b_gate, q_norm_g, w_uq, kv_norm_g, w_ukv, na_rpb, w_na_o, w_mla_o, w_out, ln1_g, ln1_b,
           ffn_w1, ffn_w3, ffn_w2, moe_w_router, moe_b_router, moe_w1, moe_w3, moe_w2, w_ple_gate, w_ple,
           ln2_g, ln2_b):
    b, s, d = x.shape
    depth = w_in.shape[0]
    alpha = (2 * depth) ** 0.25
    tl = _tiles(b * s, s)
    tabs = _rope_tables(s)

    for i in range(depth):
        w = _prep_layer(i, w_in, b_gate, q_norm_g, w_uq, kv_norm_g, w_ukv, w_na_o, w_mla_o, w_out, ln1_g, ln1_b,
                        w_ple_gate, w_ple, ln2_g, ln2_b)
        qkv, gates, q_mla, k_mla, vt = _proj_call(x, w, tabs, tl['tm'])
        o_na = _na_call(qkv, _na_bias_table(na_rpb[i]), tl['na_rows'])
        o_mla_t = _mla_call(q_mla, k_mla, vt, tl['tq'], tl['tk'], tl['mla_unroll'])
        o_mla_t = o_mla_t.reshape(b, MLA_HEADS * MLA_V, s)
        x1, x1b = _merge_call(x, o_na, o_mla_t, gates, w, alpha, tl['tm'])
        j = i // 2
        if i % 2 == 0:
            w.update(w1=ffn_w1[j].astype(BF), w3=ffn_w3[j].astype(BF), w2=ffn_w2[j].astype(BF))
            x = _ffn_call(x1, p[i], w, alpha, tl['tm'], tl['ffn_chunks'])
        else:
            wr = jnp.pad(moe_w_router[j].T, ((0, BF16_SUBLANES - N_EXPERTS), (0, 0)))
            wrh = wr.astype(BF)
            br = jnp.concatenate([moe_b_router[j], jnp.full((BF16_SUBLANES - N_EXPERTS,), MASK_VALUE, F32)])
            w.update(wrh=wrh, wrl=(wr - wrh.astype(F32)).astype(BF), br=br[:, None],
                     mw1=moe_w1[j].astype(BF), mw3=moe_w3[j].astype(BF), mw2=moe_w2[j].astype(BF))
            x = _moe_layer(x1, x1b, p[i], w, alpha, tl)
    return x
```

```python
import functools
import math

import numpy as np
import jax
import jax.numpy as jnp
from jax import lax
from jax.experimental import pallas as pl
from jax.experimental.pallas import tpu as pltpu

BF = jnp.bfloat16
F32 = jnp.float32

GRID_W = 64
NA_HEADS = 8
NA_HEAD_DIM = 64
NA_ROWS = 8
NA_COLS = 16
NA_WIDTH = NA_HEADS * NA_HEAD_DIM
MLA_HEADS = 8
MLA_NOPE = 64
MLA_ROPE = 32
MLA_V = 64
MLA_Q_LORA = 768
MLA_KV_LORA = 256
ROPE_THETA = 10000.0
N_EXPERTS = 8
LN_EPS = 1e-5
RMS_EPS = 1e-6
MASK_VALUE = -1e30

LANES = 128
BF16_SUBLANES = 16
MXU_COLS = 256
VMEM_LIMIT_BYTES = 56 * 1024 * 1024

HEAD_PAD = LANES
VT_ROWS = MLA_V + BF16_SUBLANES


def _dot(a, b):
    return jnp.dot(a, b, preferred_element_type=F32)


def _dot_nt(a, b):
    return lax.dot_general(a, b, (((1,), (1,)), ((), ())), preferred_element_type=F32)


def _dot_tn(a, b):
    return lax.dot_general(a, b, (((0,), (0,)), ((), ())), preferred_element_type=F32)


def _const_spec(shape):
    nd = len(shape)
    return pl.BlockSpec(shape, lambda *_: (0,) * nd, pipeline_mode=pl.Buffered(1))


def _params(sem):
    return pltpu.CompilerParams(dimension_semantics=sem, vmem_limit_bytes=VMEM_LIMIT_BYTES)


def _tiles(n_tokens, seq):
    return dict(
        tm=min(512, seq),
        tq=min(1024, seq),
        tk=min(256, seq),
        mla_unroll=8 if (seq // min(256, seq)) % 8 == 0 else 2,
        na_rows=8,
        moe_tm=1024 if n_tokens >= 8192 else 128,
        moe_chunks=4,
        ffn_chunks=2,
    )


def _layer_norm(y, g, b):
    mu = jnp.mean(y, axis=-1, keepdims=True)
    yc = y - mu
    var = jnp.mean(yc * yc, axis=-1, keepdims=True)
    return yc * lax.rsqrt(var + LN_EPS) * g + b


def _rms_norm(y, g):
    return y * lax.rsqrt(jnp.mean(y * y, axis=-1, keepdims=True) + RMS_EPS) * g


def _silu(y):
    return y * jax.nn.sigmoid(y)


def _proj_kernel(x_ref, wna_ref, wg_ref, bg_ref, wql_ref, wkvl_ref, wkr_ref, gq_ref, gkv_ref,
                 wuq_ref, wuk_ref, wuvt_ref, cq_ref, sq_ref, ck_ref, sk_ref,
                 qkv_ref, gate_ref, q_ref, k_ref, vt_ref):
    xb = x_ref[0].astype(BF)
    qkv_ref[0] = _dot(xb, wna_ref[...]).astype(BF)
    gate_ref[0] = jax.nn.sigmoid(_dot(xb, wg_ref[...]) + bg_ref[...]).astype(BF)

    qn = _rms_norm(_dot(xb, wql_ref[...]), gq_ref[...]).astype(BF)
    qa = _dot(qn, wuq_ref[...])
    cq, sq = cq_ref[...], sq_ref[...]
    for h in range(MLA_HEADS):
        qh = qa[:, h * HEAD_PAD:(h + 1) * HEAD_PAD]
        partner = pltpu.roll(qh, shift=HEAD_PAD - MLA_ROPE, axis=1)
        q_ref[0, h] = (qh * cq + partner * sq).astype(BF)

    kvn = _rms_norm(_dot(xb, wkvl_ref[...]), gkv_ref[...]).astype(BF)
    kr2 = _dot(xb, wkr_ref[...])
    kr = kr2[:, :HEAD_PAD] * ck_ref[...] + kr2[:, HEAD_PAD:] * sk_ref[...]
    kn = _dot(kvn, wuk_ref[...])
    for h in range(MLA_HEADS):
        sl = slice(h * HEAD_PAD, (h + 1) * HEAD_PAD)
        k_ref[0, h] = (kn[:, sl] + kr).astype(BF)

    vt = _dot_nt(wuvt_ref[...], kvn)
    ones = jnp.ones((BF16_SUBLANES, vt.shape[1]), BF)
    for h in range(MLA_HEADS):
        vt_ref[0, h, 0:MLA_V, :] = vt[h * MLA_V:(h + 1) * MLA_V].astype(BF)
        vt_ref[0, h, MLA_V:VT_ROWS, :] = ones


def _proj_call(x, w, tabs, tm):
    b, s, d = x.shape
    grid = (b, s // tm)
    weights = [w['wna'], w['wg'], w['bg'], w['wql'], w['wkvl'], w['wkr'], w['gq'], w['gkv'],
               w['wuq'], w['wuk'], w['wuvt']]
    tab_spec = pl.BlockSpec((tm, HEAD_PAD), lambda i, j: (j, 0))
    in_specs = ([pl.BlockSpec((1, tm, d), lambda i, j: (i, j, 0))]
                + [_const_spec(a.shape) for a in weights] + [tab_spec] * 4)
    out_shape = (
        jax.ShapeDtypeStruct((b, s, 3 * NA_WIDTH), BF),
        jax.ShapeDtypeStruct((b, s, 2 * d), BF),
        jax.ShapeDtypeStruct((b, MLA_HEADS, s, HEAD_PAD), BF),
        jax.ShapeDtypeStruct((b, MLA_HEADS, s, HEAD_PAD), BF),
        jax.ShapeDtypeStruct((b, MLA_HEADS, VT_ROWS, s), BF),
    )
    out_specs = (
        pl.BlockSpec((1, tm, 3 * NA_WIDTH), lambda i, j: (i, j, 0)),
        pl.BlockSpec((1, tm, 2 * d), lambda i, j: (i, j, 0)),
        pl.BlockSpec((1, MLA_HEADS, tm, HEAD_PAD), lambda i, j: (i, 0, j, 0)),
        pl.BlockSpec((1, MLA_HEADS, tm, HEAD_PAD), lambda i, j: (i, 0, j, 0)),
        pl.BlockSpec((1, MLA_HEADS, VT_ROWS, tm), lambda i, j: (i, 0, 0, j)),
    )
    return pl.pallas_call(
        _proj_kernel, grid=grid, in_specs=in_specs, out_specs=out_specs, out_shape=out_shape,
        compiler_params=_params(("parallel", "parallel")), name="proj",
    )(x, *weights, *tabs)


def _na_kernel(q_ref, kp_ref, kc_ref, kn_ref, vp_ref, vc_ref, vn_ref, bias_ref, o_ref,
               kcat_ref, vcat_ref, *, rows, rb):
    i = pl.program_id(1)
    blk = rb * GRID_W
    win = NA_ROWS * GRID_W
    kcat_ref[0:blk] = kp_ref[0]
    kcat_ref[blk:2 * blk] = kc_ref[0]
    kcat_ref[2 * blk:3 * blk] = kn_ref[0]
    vcat_ref[0:blk] = vp_ref[0]
    vcat_ref[blk:2 * blk] = vc_ref[0]
    vcat_ref[2 * blk:3 * blk] = vn_ref[0]
    lane = lax.broadcasted_iota(jnp.int32, (GRID_W, LANES), 1)
    first_head = lane < NA_HEAD_DIM

    def row(a, carry):
        r = i * rb + a
        rs = jnp.clip(r - NA_ROWS // 2, 0, rows - NA_ROWS)
        off = pl.multiple_of((rs - (i - 1) * rb) * GRID_W, GRID_W)
        d = rs - r + (NA_ROWS - 1)
        qoff = pl.multiple_of(a * GRID_W, GRID_W)
        qrow = q_ref[0, pl.ds(qoff, GRID_W), :]
        zero = jnp.zeros((GRID_W, LANES), BF)
        sc = []
        for hp in range(NA_HEADS // 2):
            sl = slice(hp * LANES, (hp + 1) * LANES)
            qp = qrow[:, sl]
            qm = jnp.concatenate([jnp.where(first_head, qp, zero), jnp.where(first_head, zero, qp)], axis=0)
            sc.append(_dot_nt(qm, kcat_ref[pl.ds(off, win), sl]))
        sc = jnp.concatenate(sc, axis=0) + bias_ref[d]
        m = jnp.max(sc, axis=-1, keepdims=True)
        p = jnp.exp(sc - m)
        inv_l = 1.0 / jnp.sum(p, axis=-1, keepdims=True)
        pb = p.astype(BF)
        for hp in range(NA_HEADS // 2):
            sl = slice(hp * LANES, (hp + 1) * LANES)
            rows2 = slice(hp * 2 * GRID_W, (hp + 1) * 2 * GRID_W)
            o2 = _dot(pb[rows2], vcat_ref[pl.ds(off, win), sl]) * inv_l[rows2]
            o_ref[0, pl.ds(qoff, GRID_W), sl] = jnp.where(first_head, o2[:GRID_W], o2[GRID_W:]).astype(BF)
        return carry

    lax.fori_loop(0, rb, row, 0, unroll=2)


def _na_call(qkv, bias_tab, rb):
    b, s, _ = qkv.shape
    rows = s // GRID_W
    nb = rows // rb
    blk = rb * GRID_W

    def kv_spec(col, shift):
        return pl.BlockSpec((1, blk, NA_WIDTH),
                            lambda i, j: (i, jnp.clip(j + shift, 0, nb - 1), col))

    in_specs = [pl.BlockSpec((1, blk, NA_WIDTH), lambda i, j: (i, j, 0)),
                kv_spec(1, -1), kv_spec(1, 0), kv_spec(1, 1),
                kv_spec(2, -1), kv_spec(2, 0), kv_spec(2, 1),
                _const_spec(bias_tab.shape)]
    return pl.pallas_call(
        functools.partial(_na_kernel, rows=rows, rb=rb),
        grid=(b, nb), in_specs=in_specs,
        out_specs=pl.BlockSpec((1, blk, NA_WIDTH), lambda i, j: (i, j, 0)),
        out_shape=jax.ShapeDtypeStruct((b, s, NA_WIDTH), BF),
        scratch_shapes=[pltpu.VMEM((3 * blk, NA_WIDTH), BF), pltpu.VMEM((3 * blk, NA_WIDTH), BF)],
        compiler_params=_params(("parallel", "parallel")), name="na_attn",
    )(qkv, qkv, qkv, qkv, qkv, qkv, qkv, bias_tab)


def _na_bias_table(rpb):
    qc = np.arange(GRID_W)[:, None]
    kc = np.arange(GRID_W)[None, :]
    win_start = np.clip(qc - NA_COLS // 2, 0, GRID_W - NA_COLS)
    valid = (kc >= win_start) & (kc < win_start + NA_COLS)
    dc = np.clip(kc - qc + NA_COLS - 1, 0, 2 * NA_COLS - 2)
    tab = jnp.where(jnp.asarray(valid)[None, None], rpb[:, :, dc], MASK_VALUE)
    t = jnp.stack([tab[:, d:d + NA_ROWS] for d in range(NA_ROWS)], axis=0)
    t = t.transpose(0, 1, 3, 2, 4)
    return t.reshape(NA_ROWS, NA_HEADS * GRID_W, NA_ROWS * GRID_W).astype(F32)


def _mla_kernel(q_ref, k_ref, vt_ref, o_ref, s0_ref, s1_ref, mx_ref, acc_ref, m_ref, *, tk, unroll):
    q = q_ref[0, 0]
    nk = k_ref.shape[2] // tk
    s_refs = (s0_ref, s1_ref)

    def key_slice(c):
        return pl.ds(pl.multiple_of(c * tk, tk), tk)

    def scores(c, slot):
        st = _dot_nt(k_ref[0, 0, key_slice(c), :], q)
        s_refs[slot][...] = st
        mx_ref[slot] = jnp.max(st, axis=0, keepdims=True)

    def consume(c, slot):
        s_ref = s_refs[slot]
        vt = vt_ref[0, 0, :, key_slice(c)]
        for j in range(q.shape[0] // MXU_COLS):
            cs = slice(j * MXU_COLS, (j + 1) * MXU_COLS)
            m = m_ref[:, cs]
            m_new = jnp.maximum(m, mx_ref[slot, :, cs])
            p = jnp.exp2(s_ref[:, cs] - m_new).astype(BF)
            acc_ref[:, cs] = acc_ref[:, cs] * jnp.exp2(m - m_new) + _dot(vt, p)
            m_ref[:, cs] = m_new

    scores(0, 0)
    m_ref[...] = jnp.full(m_ref.shape, MASK_VALUE, F32)
    acc_ref[...] = jnp.zeros(acc_ref.shape, F32)

    def body(i, carry):
        c0 = unroll * i
        for u in range(unroll):
            scores(c0 + u + 1, (u + 1) % 2)
            consume(c0 + u, u % 2)
        return carry

    lax.fori_loop(0, nk // unroll - 1, body, 0)
    c0 = nk - unroll
    for u in range(unroll):
        if u + 1 < unroll:
            scores(c0 + u + 1, (u + 1) % 2)
        consume(c0 + u, u % 2)
    acc = acc_ref[...]
    o_ref[0, 0] = (acc[0:MLA_V] / acc[MLA_V:MLA_V + 1]).astype(BF)


def _mla_call(q, k, vt, tq, tk, unroll):
    b, h, s, _ = q.shape
    assert unroll % 2 == 0 and (s // tk) % unroll == 0
    return pl.pallas_call(
        functools.partial(_mla_kernel, tk=tk, unroll=unroll),
        grid=(b, h, s // tq),
        in_specs=[pl.BlockSpec((1, 1, tq, HEAD_PAD), lambda i, j, t: (i, j, t, 0)),
                  pl.BlockSpec((1, 1, s, HEAD_PAD), lambda i, j, t: (i, j, 0, 0)),
                  pl.BlockSpec((1, 1, VT_ROWS, s), lambda i, j, t: (i, j, 0, 0))],
        out_specs=pl.BlockSpec((1, 1, MLA_V, tq), lambda i, j, t: (i, j, 0, t)),
        out_shape=jax.ShapeDtypeStruct((b, h, MLA_V, s), BF),
        scratch_shapes=[pltpu.VMEM((tk, tq), F32), pltpu.VMEM((tk, tq), F32), pltpu.VMEM((2, 1, tq), F32),
                        pltpu.VMEM((VT_ROWS, tq), F32), pltpu.VMEM((1, tq), F32)],
        compiler_params=_params(("parallel", "parallel", "parallel")), name="mla_attn",
    )(q, k, vt)


def _merge_kernel(x_ref, ona_ref, omt_ref, gate_ref, wna_ref, wmla_ref, wout_ref, g_ref, b_ref,
                  o_ref, ob_ref, *, alpha):
    d = x_ref.shape[2]
    a = _dot(ona_ref[0], wna_ref[...])
    m = _dot_tn(omt_ref[0], wmla_ref[...])
    gate = gate_ref[0]
    merged = gate[:, :d].astype(F32) * a + gate[:, d:].astype(F32) * m
    y = alpha * x_ref[0] + _dot(merged.astype(BF), wout_ref[...])
    out = _layer_norm(y, g_ref[...], b_ref[...])
    o_ref[0] = out
    ob_ref[0] = out.astype(BF)


def _merge_call(x, o_na, o_mla_t, gates, w, alpha, tm):
    b, s, d = x.shape
    weights = [w['wnao'], w['wmlao'], w['wout'], w['ln1g'], w['ln1b']]
    in_specs = [pl.BlockSpec((1, tm, d), lambda i, j: (i, j, 0)),
                pl.BlockSpec((1, tm, NA_WIDTH), lambda i, j: (i, j, 0)),
                pl.BlockSpec((1, MLA_HEADS * MLA_V, tm), lambda i, j: (i, 0, j)),
                pl.BlockSpec((1, tm, 2 * d), lambda i, j: (i, j, 0))] + [_const_spec(a.shape) for a in weights]
    return pl.pallas_call(
        functools.partial(_merge_kernel, alpha=alpha),
        grid=(b, s // tm), in_specs=in_specs,
        out_specs=(pl.BlockSpec((1, tm, d), lambda i, j: (i, j, 0)),
                   pl.BlockSpec((1, tm, d), lambda i, j: (i, j, 0))),
        out_shape=(jax.ShapeDtypeStruct((b, s, d), F32), jax.ShapeDtypeStruct((b, s, d), BF)),
        compiler_params=_params(("parallel", "parallel")), name="merge_ln1",
    )(x, o_na, o_mla_t, gates, *weights)


def _ple_ln2(x, xb, f, p_ref, wpg_ref, wp_ref, g_ref, b_ref, alpha):
    ple = jax.nn.sigmoid(_dot(xb, wpg_ref[...])) * _dot(p_ref[0].astype(BF), wp_ref[...])
    return _layer_norm(alpha * x + f + ple, g_ref[...], b_ref[...])


def _ffn_kernel(x_ref, p_ref, w1_ref, w3_ref, w2_ref, wpg_ref, wp_ref, g_ref, b_ref, o_ref,
                *, alpha, chunks):
    x = x_ref[0]
    xb = x.astype(BF)
    fc = w1_ref.shape[1] // chunks
    f = None
    for c in range(chunks):
        sl = slice(c * fc, (c + 1) * fc)
        h = _silu(_dot(xb, w1_ref[:, sl])) * _dot(xb, w3_ref[:, sl])
        part = _dot(h.astype(BF), w2_ref[sl, :])
        f = part if f is None else f + part
    o_ref[0] = _ple_ln2(x, xb, f, p_ref, wpg_ref, wp_ref, g_ref, b_ref, alpha)


def _ffn_call(x, p, w, alpha, tm, chunks):
    b, s, d = x.shape
    weights = [w['w1'], w['w3'], w['w2'], w['wpg'], w['wp'], w['ln2g'], w['ln2b']]
    in_specs = [pl.BlockSpec((1, tm, d), lambda i, j: (i, j, 0)),
                pl.BlockSpec((1, tm, p.shape[-1]), lambda i, j: (i, j, 0))] + [_const_spec(a.shape) for a in weights]
    return pl.pallas_call(
        functools.partial(_ffn_kernel, alpha=alpha, chunks=chunks),
        grid=(b, s // tm), in_specs=in_specs,
        out_specs=pl.BlockSpec((1, tm, d), lambda i, j: (i, j, 0)),
        out_shape=jax.ShapeDtypeStruct((b, s, d), F32),
        compiler_params=_params(("parallel", "parallel")), name="ffn_ple_ln2",
    )(x, p, *weights)


def _router_kernel(x_ref, wh_ref, wl_ref, b_ref, o_ref):
    x = x_ref[...]
    xh = x.astype(BF)
    xl = (x - xh.astype(F32)).astype(BF)
    lg = _dot_nt(wh_ref[...], xh) + _dot_nt(wh_ref[...], xl) + _dot_nt(wl_ref[...], xh) + b_ref[...]
    rid = lax.broadcasted_iota(jnp.int32, lg.shape, 0)
    pad = lg.shape[0]
    m1 = jnp.max(lg, axis=0, keepdims=True)
    i1 = jnp.min(jnp.where(lg == m1, rid, pad), axis=0, keepdims=True)
    lg2 = jnp.where(rid == i1, MASK_VALUE, lg)
    m2 = jnp.max(lg2, axis=0, keepdims=True)
    i2 = jnp.min(jnp.where(lg2 == m2, rid, pad), axis=0, keepdims=True)
    e = jnp.exp(m2 - m1)
    o_ref[0:1, :] = i1.astype(F32)
    o_ref[1:2, :] = i2.astype(F32)
    o_ref[2:3, :] = 1.0 / (1.0 + e)
    o_ref[3:4, :] = e / (1.0 + e)
    o_ref[4:8, :] = jnp.zeros((4, lg.shape[1]), F32)


def _router_call(x2, wh, wl, bias, tm):
    n, d = x2.shape
    return pl.pallas_call(
        _router_kernel, grid=(n // tm,),
        in_specs=[pl.BlockSpec((tm, d), lambda i: (i, 0)), _const_spec(wh.shape), _const_spec(wl.shape),
                  _const_spec(bias.shape)],
        out_specs=pl.BlockSpec((8, tm), lambda i: (0, i)),
        out_shape=jax.ShapeDtypeStruct((8, n), F32),
        compiler_params=_params(("parallel",)), name="router",
    )(x2, wh, wl, bias)


def _experts_kernel(te_ref, na_ref, xs_ref, w1_ref, w3_ref, w2_ref, o_ref, acc_ref, *, chunks):
    t = pl.program_id(0)
    c = pl.program_id(1)
    active = t < na_ref[0]

    @pl.when(active)
    def _():
        xb = xs_ref[...]
        h = _silu(_dot(xb, w1_ref[0])) * _dot(xb, w3_ref[0])
        part = _dot(h.astype(BF), w2_ref[0])

        @pl.when(c == 0)
        def _():
            acc_ref[...] = part

        @pl.when(c > 0)
        def _():
            acc_ref[...] += part

        @pl.when(c == chunks - 1)
        def _():
            o_ref[...] = acc_ref[...].astype(BF)

    @pl.when(jnp.logical_and(jnp.logical_not(active), c == chunks - 1))
    def _():
        o_ref[...] = jnp.zeros_like(o_ref)


def _experts_call(tile_expert, n_active, xs, w1, w3, w2, tm, chunks):
    p_rows, d = xs.shape
    fc = w1.shape[2] // chunks
    grid_spec = pltpu.PrefetchScalarGridSpec(
        num_scalar_prefetch=2, grid=(p_rows // tm, chunks),
        in_specs=[pl.BlockSpec((tm, d), lambda t, c, te, na: (t, 0)),
                  pl.BlockSpec((1, d, fc), lambda t, c, te, na: (te[t], 0, c)),
                  pl.BlockSpec((1, d, fc), lambda t, c, te, na: (te[t], 0, c)),
                  pl.BlockSpec((1, fc, d), lambda t, c, te, na: (te[t], c, 0))],
        out_specs=pl.BlockSpec((tm, d), lambda t, c, te, na: (t, 0)),
        scratch_shapes=[pltpu.VMEM((tm, d), F32)])
    return pl.pallas_call(
        functools.partial(_experts_kernel, chunks=chunks), grid_spec=grid_spec,
        out_shape=jax.ShapeDtypeStruct((p_rows, d), BF),
        compiler_params=_params(("arbitrary", "arbitrary")), name="experts",
    )(tile_expert, n_active, xs, w1, w3, w2)


def _combine_kernel(x_ref, ya_ref, yb_ref, gw_ref, p_ref, wpg_ref, wp_ref, g_ref, b_ref, o_ref, *, alpha):
    x = x_ref[0]
    gw = gw_ref[0]
    f = gw[:, 0:1] * ya_ref[0].astype(F32) + gw[:, 1:2] * yb_ref[0].astype(F32)
    o_ref[0] = _ple_ln2(x, x.astype(BF), f, p_ref, wpg_ref, wp_ref, g_ref, b_ref, alpha)


def _combine_call(x, ya, yb, gw, p, w, alpha, tm):
    b, s, d = x.shape
    weights = [w['wpg'], w['wp'], w['ln2g'], w['ln2b']]
    tok = lambda width: pl.BlockSpec((1, tm, width), lambda i, j: (i, j, 0))
    in_specs = [tok(d), tok(d), tok(d), tok(2), tok(p.shape[-1])] + [_const_spec(a.shape) for a in weights]
    return pl.pallas_call(
        functools.partial(_combine_kernel, alpha=alpha),
        grid=(b, s // tm), in_specs=in_specs, out_specs=tok(d),
        out_shape=jax.ShapeDtypeStruct((b, s, d), F32),
        compiler_params=_params(("parallel", "parallel")), name="moe_combine_ple_ln2",
    )(x, ya, yb, gw, p, *weights)


def _moe_layer(x1, x1b, p, w, alpha, tl):
    b, s, d = x1.shape
    n = b * s
    tm = tl['moe_tm']
    route = _router_call(x1.reshape(n, d), w['wrh'], w['wrl'], w['br'], tl['tm'])
    e_all = route[0:2].astype(jnp.int32).reshape(2 * n)
    gw = route[2:4].T.reshape(b, s, 2)

    onehot = (e_all[:, None] == jnp.arange(N_EXPERTS, dtype=jnp.int32)[None, :]).astype(jnp.int32)
    csum = jnp.cumsum(onehot, axis=0)
    rank = jnp.sum(onehot * (csum - 1), axis=1)
    counts = csum[-1]
    padded = ((counts + tm - 1) // tm) * tm
    ends = jnp.cumsum(padded)
    starts = ends - padded
    dest = starts[e_all] + rank
    p_rows = 2 * n + N_EXPERTS * tm
    n_active = (ends[-1] // tm).astype(jnp.int32).reshape(1)
    tile_start = jnp.minimum(jnp.arange(p_rows // tm, dtype=jnp.int32), n_active[0] - 1) * tm
    tile_expert = jnp.sum((tile_start[:, None] >= ends[None, :]).astype(jnp.int32), axis=1)
    tile_expert = jnp.minimum(tile_expert, N_EXPERTS - 1).astype(jnp.int32)
    order = jnp.argsort(e_all, stable=True).astype(jnp.int32)
    row_expert = jnp.repeat(tile_expert, tm)
    row_rank = jnp.arange(p_rows, dtype=jnp.int32) - starts[row_expert]
    src = jnp.clip((jnp.cumsum(counts) - counts)[row_expert] + row_rank, 0, 2 * n - 1)
    row_token = jnp.where(row_rank < counts[row_expert], order[src] % n, 0)

    xs = jnp.take(x1b.reshape(n, d), row_token, axis=0)
    ys = _experts_call(tile_expert, n_active, xs, w['mw1'], w['mw3'], w['mw2'], tm, tl['moe_chunks'])
    ya = jnp.take(ys, dest[:n], axis=0).reshape(b, s, d)
    yb = jnp.take(ys, dest[n:], axis=0).reshape(b, s, d)
    return _combine_call(x1, ya, yb, gw, p, w, alpha, tl['tm'])


def _rope_tables(s):
    pos = jnp.arange(s, dtype=F32)
    inv_freq = ROPE_THETA ** (-jnp.arange(0, MLA_ROPE // 2, dtype=F32) * (2.0 / MLA_ROPE))
    ang = pos[:, None] * inv_freq[None, :]
    cos, sin = jnp.cos(ang), jnp.sin(ang)
    ck = jnp.concatenate([jnp.ones((s, MLA_NOPE), F32), cos, cos,
                          jnp.zeros((s, HEAD_PAD - MLA_NOPE - MLA_ROPE), F32)], axis=1)
    sk = jnp.concatenate([jnp.zeros((s, MLA_NOPE), F32), sin, sin,
                          jnp.zeros((s, HEAD_PAD - MLA_NOPE - MLA_ROPE), F32)], axis=1)
    qs = (MLA_NOPE + MLA_ROPE) ** -0.5 * math.log2(math.e)
    return ck * qs, sk * qs, ck, sk


def _rotate_half_cols(w_rope):
    half = w_rope.shape[-1] // 2
    return jnp.concatenate([-w_rope[..., half:], w_rope[..., :half]], axis=-1)


def _prep_layer(i, w_in, b_gate, q_norm_g, w_uq, kv_norm_g, w_ukv, w_na_o, w_mla_o, w_out, ln1_g, ln1_b,
                w_ple_gate, w_ple, ln2_g, ln2_b):
    d = w_in.shape[1]
    o = 0
    wna = w_in[i, :, o:o + 3 * NA_WIDTH]; o += 3 * NA_WIDTH
    wql = w_in[i, :, o:o + MLA_Q_LORA]; o += MLA_Q_LORA
    wkvl = w_in[i, :, o:o + MLA_KV_LORA]; o += MLA_KV_LORA
    wkr = w_in[i, :, o:o + MLA_ROPE]; o += MLA_ROPE
    wg = w_in[i, :, o:o + 2 * d]
    na_scale = jnp.concatenate([jnp.full((NA_WIDTH,), NA_HEAD_DIM ** -0.5, F32), jnp.ones((2 * NA_WIDTH,), F32)])
    wna = wna * na_scale[None, :]

    zpad = HEAD_PAD - MLA_NOPE - MLA_ROPE
    uq = w_uq[i].reshape(MLA_Q_LORA, MLA_HEADS, MLA_NOPE + MLA_ROPE)
    assert zpad == MLA_ROPE
    uq_pad = jnp.concatenate([uq, _rotate_half_cols(uq[..., MLA_NOPE:])], axis=-1)
    kr_pad = jnp.pad(wkr, ((0, 0), (MLA_NOPE, zpad)))
    kr_rot = jnp.pad(_rotate_half_cols(wkr), ((0, 0), (MLA_NOPE, zpad)))
    ukv = w_ukv[i].reshape(MLA_KV_LORA, MLA_HEADS, MLA_NOPE + MLA_V)
    uk_pad = jnp.pad(ukv[..., :MLA_NOPE], ((0, 0), (0, 0), (0, HEAD_PAD - MLA_NOPE)))
    uvt = ukv[..., MLA_NOPE:].reshape(MLA_KV_LORA, MLA_HEADS * MLA_V).T

    return dict(
        wna=wna.astype(BF), wg=wg.astype(BF), bg=b_gate[i][None, :], wql=wql.astype(BF), wkvl=wkvl.astype(BF),
        wkr=jnp.concatenate([kr_pad, kr_rot], axis=1).astype(BF),
        gq=q_norm_g[i][None, :], gkv=kv_norm_g[i][None, :],
        wuq=uq_pad.reshape(MLA_Q_LORA, MLA_HEADS * HEAD_PAD).astype(BF),
        wuk=uk_pad.reshape(MLA_KV_LORA, MLA_HEADS * HEAD_PAD).astype(BF), wuvt=uvt.astype(BF),
        wnao=w_na_o[i].astype(BF), wmlao=w_mla_o[i].astype(BF), wout=w_out[i].astype(BF),
        ln1g=ln1_g[i][None, :], ln1b=ln1_b[i][None, :],
        wpg=w_ple_gate[i].astype(BF), wp=w_ple[i].astype(BF), ln2g=ln2_g[i][None, :], ln2b=ln2_b[i][None, :],
    )


def kernel(x, p, w_in, b_gate, q_norm_g, w_uq, kv_norm_g, w_ukv, na_rpb, w_na_o, w_mla_o, w_out, ln1_g, ln1_b,
           ffn_w1, ffn_w3, ffn_w2, moe_w_router, moe_b_router, moe_w1, moe_w3, moe_w2, w_ple_gate, w_ple,
           ln2_g, ln2_b):
    b, s, d = x.shape
    depth = w_in.shape[0]
    alpha = (2 * depth) ** 0.25
    tl = _tiles(b * s, s)
    tabs = _rope_tables(s)

    for i in range(depth):
        w = _prep_layer(i, w_in, b_gate, q_norm_g, w_uq, kv_norm_g, w_ukv, w_na_o, w_mla_o, w_out, ln1_g, ln1_b,
                        w_ple_gate, w_ple, ln2_g, ln2_b)
        qkv, gates, q_mla, k_mla, vt = _proj_call(x, w, tabs, tl['tm'])
        o_na = _na_call(qkv, _na_bias_table(na_rpb[i]), tl['na_rows'])
        o_mla_t = _mla_call(q_mla, k_mla, vt, tl['tq'], tl['tk'], tl['mla_unroll'])
        o_mla_t = o_mla_t.reshape(b, MLA_HEADS * MLA_V, s)
        x1, x1b = _merge_call(x, o_na, o_mla_t, gates, w, alpha, tl['tm'])
        j = i // 2
        if i % 2 == 0:
            w.update(w1=ffn_w1[j].astype(BF), w3=ffn_w3[j].astype(BF), w2=ffn_w2[j].astype(BF))
            x = _ffn_call(x1, p[i], w, alpha, tl['tm'], tl['ffn_chunks'])
        else:
            wr = jnp.pad(moe_w_router[j].T, ((0, BF16_SUBLANES - N_EXPERTS), (0, 0)))
            wrh = wr.astype(BF)
            br = jnp.concatenate([moe_b_router[j], jnp.full((BF16_SUBLANES - N_EXPERTS,), MASK_VALUE, F32)])
            w.update(wrh=wrh, wrl=(wr - wrh.astype(F32)).astype(BF), br=br[:, None],
                     mw1=moe_w1[j].astype(BF), mw3=moe_w3[j].astype(BF), mw2=moe_w2[j].astype(BF))
            x = _moe_layer(x1, x1b, p[i], w, alpha, tl)
    return x
```

```python
import functools
import math

import numpy as np
import jax
import jax.numpy as jnp
from jax import lax
from jax.experimental import pallas as pl
from jax.experimental.pallas import tpu as pltpu

BF = jnp.bfloat16
F32 = jnp.float32

GRID_W = 64
NA_HEADS = 8
NA_HEAD_DIM = 64
NA_ROWS = 8
NA_COLS = 16
NA_WIDTH = NA_HEADS * NA_HEAD_DIM
MLA_HEADS = 8
MLA_NOPE = 64
MLA_ROPE = 32
MLA_V = 64
MLA_Q_LORA = 768
MLA_KV_LORA = 256
ROPE_THETA = 10000.0
N_EXPERTS = 8
LN_EPS = 1e-5
RMS_EPS = 1e-6
MASK_VALUE = -1e30

LANES = 128
BF16_SUBLANES = 16
MXU_COLS = 256
VMEM_LIMIT_BYTES = 56 * 1024 * 1024

HEAD_PAD = LANES
VT_ROWS = MLA_V + BF16_SUBLANES


def _dot(a, b):
    return jnp.dot(a, b, preferred_element_type=F32)


def _dot_nt(a, b):
    return lax.dot_general(a, b, (((1,), (1,)), ((), ())), preferred_element_type=F32)


def _dot_tn(a, b):
    return lax.dot_general(a, b, (((0,), (0,)), ((), ())), preferred_element_type=F32)


def _const_spec(shape):
    nd = len(shape)
    return pl.BlockSpec(shape, lambda *_: (0,) * nd, pipeline_mode=pl.Buffered(1))


def _params(sem):
    return pltpu.CompilerParams(dimension_semantics=sem, vmem_limit_bytes=VMEM_LIMIT_BYTES)


def _tiles(n_tokens, seq):
    return dict(
        tm=min(512, seq),
        tq=min(1024, seq),
        tk=min(256, seq),
        mla_unroll=8 if (seq // min(256, seq)) % 8 == 0 else 2,
        na_rows=8,
        moe_tm=1024 if n_tokens >= 8192 else 128,
        moe_chunks=4,
        ffn_chunks=2,
    )


def _layer_norm(y, g, b):
    mu = jnp.mean(y, axis=-1, keepdims=True)
    yc = y - mu
    var = jnp.mean(yc * yc, axis=-1, keepdims=True)
    return yc * lax.rsqrt(var + LN_EPS) * g + b


def _rms_norm(y, g):
    return y * lax.rsqrt(jnp.mean(y * y, axis=-1, keepdims=True) + RMS_EPS) * g


def _silu(y):
    return y * jax.nn.sigmoid(y)


def _proj_kernel(x_ref, wna_ref, wg_ref, bg_ref, wql_ref, wkvl_ref, wkr_ref, gq_ref, gkv_ref,
                 wuq_ref, wuk_ref, wuvt_ref, cq_ref, sq_ref, ck_ref, sk_ref,
                 qkv_ref, gate_ref, q_ref, k_ref, vt_ref):
    xb = x_ref[0].astype(BF)
    qkv_ref[0] = _dot(xb, wna_ref[...]).astype(BF)
    gate_ref[0] = jax.nn.sigmoid(_dot(xb, wg_ref[...]) + bg_ref[...]).astype(BF)

    qn = _rms_norm(_dot(xb, wql_ref[...]), gq_ref[...]).astype(BF)
    qat = _dot_nt(wuq_ref[...], qn)
    cq, sq = cq_ref[...], sq_ref[...]
    for h in range(MLA_HEADS):
        qh = qat[h * HEAD_PAD:(h + 1) * HEAD_PAD]
        partner = pltpu.roll(qh, shift=HEAD_PAD - MLA_ROPE, axis=0)
        q_ref[0, h] = (qh * cq + partner * sq).astype(BF)

    kvn = _rms_norm(_dot(xb, wkvl_ref[...]), gkv_ref[...]).astype(BF)
    kr2 = _dot(xb, wkr_ref[...])
    kr = kr2[:, :HEAD_PAD] * ck_ref[...] + kr2[:, HEAD_PAD:] * sk_ref[...]
    kn = _dot(kvn, wuk_ref[...])
    for h in range(MLA_HEADS):
        sl = slice(h * HEAD_PAD, (h + 1) * HEAD_PAD)
        k_ref[0, h] = (kn[:, sl] + kr).astype(BF)

    vt = _dot_nt(wuvt_ref[...], kvn)
    ones = jnp.ones((BF16_SUBLANES, vt.shape[1]), BF)
    for h in range(MLA_HEADS):
        vt_ref[0, h, 0:MLA_V, :] = vt[h * MLA_V:(h + 1) * MLA_V].astype(BF)
        vt_ref[0, h, MLA_V:VT_ROWS, :] = ones


def _proj_call(x, w, tabs, tm):
    b, s, d = x.shape
    grid = (b, s // tm)
    weights = [w['wna'], w['wg'], w['bg'], w['wql'], w['wkvl'], w['wkr'], w['gq'], w['gkv'],
               w['wuq'], w['wuk'], w['wuvt']]
    tab_spec = pl.BlockSpec((tm, HEAD_PAD), lambda i, j: (j, 0))
    tab_t_spec = pl.BlockSpec((HEAD_PAD, tm), lambda i, j: (0, j))
    in_specs = ([pl.BlockSpec((1, tm, d), lambda i, j: (i, j, 0))]
                + [_const_spec(a.shape) for a in weights] + [tab_t_spec] * 2 + [tab_spec] * 2)
    out_shape = (
        jax.ShapeDtypeStruct((b, s, 3 * NA_WIDTH), BF),
        jax.ShapeDtypeStruct((b, s, 2 * d), BF),
        jax.ShapeDtypeStruct((b, MLA_HEADS, HEAD_PAD, s), BF),
        jax.ShapeDtypeStruct((b, MLA_HEADS, s, HEAD_PAD), BF),
        jax.ShapeDtypeStruct((b, MLA_HEADS, VT_ROWS, s), BF),
    )
    out_specs = (
        pl.BlockSpec((1, tm, 3 * NA_WIDTH), lambda i, j: (i, j, 0)),
        pl.BlockSpec((1, tm, 2 * d), lambda i, j: (i, j, 0)),
        pl.BlockSpec((1, MLA_HEADS, HEAD_PAD, tm), lambda i, j: (i, 0, 0, j)),
        pl.BlockSpec((1, MLA_HEADS, tm, HEAD_PAD), lambda i, j: (i, 0, j, 0)),
        pl.BlockSpec((1, MLA_HEADS, VT_ROWS, tm), lambda i, j: (i, 0, 0, j)),
    )
    return pl.pallas_call(
        _proj_kernel, grid=grid, in_specs=in_specs, out_specs=out_specs, out_shape=out_shape,
        compiler_params=_params(("parallel", "parallel")), name="proj",
    )(x, *weights, *tabs)


def _na_kernel(q_ref, kp_ref, kc_ref, kn_ref, vp_ref, vc_ref, vn_ref, bias_ref, o_ref,
               kcat_ref, vcat_ref, *, rows, rb):
    i = pl.program_id(1)
    blk = rb * GRID_W
    win = NA_ROWS * GRID_W
    kcat_ref[0:blk] = kp_ref[0]
    kcat_ref[blk:2 * blk] = kc_ref[0]
    kcat_ref[2 * blk:3 * blk] = kn_ref[0]
    vcat_ref[0:blk] = vp_ref[0]
    vcat_ref[blk:2 * blk] = vc_ref[0]
    vcat_ref[2 * blk:3 * blk] = vn_ref[0]
    lane = lax.broadcasted_iota(jnp.int32, (GRID_W, LANES), 1)
    first_head = lane < NA_HEAD_DIM

    def row(a, carry):
        r = i * rb + a
        rs = jnp.clip(r - NA_ROWS // 2, 0, rows - NA_ROWS)
        off = pl.multiple_of((rs - (i - 1) * rb) * GRID_W, GRID_W)
        d = rs - r + (NA_ROWS - 1)
        qoff = pl.multiple_of(a * GRID_W, GRID_W)
        qrow = q_ref[0, pl.ds(qoff, GRID_W), :]
        zero = jnp.zeros((GRID_W, LANES), BF)
        sc = []
        for hp in range(NA_HEADS // 2):
            sl = slice(hp * LANES, (hp + 1) * LANES)
            qp = qrow[:, sl]
            qm = jnp.concatenate([jnp.where(first_head, qp, zero), jnp.where(first_head, zero, qp)], axis=0)
            sc.append(_dot_nt(qm, kcat_ref[pl.ds(off, win), sl]))
        sc = jnp.concatenate(sc, axis=0) + bias_ref[d]
        m = jnp.max(sc, axis=-1, keepdims=True)
        p = jnp.exp(sc - m)
        inv_l = 1.0 / jnp.sum(p, axis=-1, keepdims=True)
        pb = p.astype(BF)
        for hp in range(NA_HEADS // 2):
            sl = slice(hp * LANES, (hp + 1) * LANES)
            rows2 = slice(hp * 2 * GRID_W, (hp + 1) * 2 * GRID_W)
            o2 = _dot(pb[rows2], vcat_ref[pl.ds(off, win), sl]) * inv_l[rows2]
            o_ref[0, pl.ds(qoff, GRID_W), sl] = jnp.where(first_head, o2[:GRID_W], o2[GRID_W:]).astype(BF)
        return carry

    lax.fori_loop(0, rb, row, 0, unroll=2)


def _na_call(qkv, bias_tab, rb):
    b, s, _ = qkv.shape
    rows = s // GRID_W
    nb = rows // rb
    blk = rb * GRID_W

    def kv_spec(col, shift):
        return pl.BlockSpec((1, blk, NA_WIDTH),
                            lambda i, j: (i, jnp.clip(j + shift, 0, nb - 1), col))

    in_specs = [pl.BlockSpec((1, blk, NA_WIDTH), lambda i, j: (i, j, 0)),
                kv_spec(1, -1), kv_spec(1, 0), kv_spec(1, 1),
                kv_spec(2, -1), kv_spec(2, 0), kv_spec(2, 1),
                _const_spec(bias_tab.shape)]
    return pl.pallas_call(
        functools.partial(_na_kernel, rows=rows, rb=rb),
        grid=(b, nb), in_specs=in_specs,
        out_specs=pl.BlockSpec((1, blk, NA_WIDTH), lambda i, j: (i, j, 0)),
        out_shape=jax.ShapeDtypeStruct((b, s, NA_WIDTH), BF),
        scratch_shapes=[pltpu.VMEM((3 * blk, NA_WIDTH), BF), pltpu.VMEM((3 * blk, NA_WIDTH), BF)],
        compiler_params=_params(("parallel", "parallel")), name="na_attn",
    )(qkv, qkv, qkv, qkv, qkv, qkv, qkv, bias_tab)


def _na_bias_table(rpb):
    qc = np.arange(GRID_W)[:, None]
    kc = np.arange(GRID_W)[None, :]
    win_start = np.clip(qc - NA_COLS // 2, 0, GRID_W - NA_COLS)
    valid = (kc >= win_start) & (kc < win_start + NA_COLS)
    dc = np.clip(kc - qc + NA_COLS - 1, 0, 2 * NA_COLS - 2)
    tab = jnp.where(jnp.asarray(valid)[None, None], rpb[:, :, dc], MASK_VALUE)
    t = jnp.stack([tab[:, d:d + NA_ROWS] for d in range(NA_ROWS)], axis=0)
    t = t.transpose(0, 1, 3, 2, 4)
    return t.reshape(NA_ROWS, NA_HEADS * GRID_W, NA_ROWS * GRID_W).astype(F32)


def _mla_kernel(q_ref, k_ref, vt_ref, o_ref, s0_ref, s1_ref, mx_ref, acc_ref, m_ref, *, tk, unroll):
    qt = q_ref[0, 0]
    tq = qt.shape[1]
    nk = k_ref.shape[2] // tk
    s_refs = (s0_ref, s1_ref)

    def key_slice(c):
        return pl.ds(pl.multiple_of(c * tk, tk), tk)

    def scores(c, slot):
        st = _dot(k_ref[0, 0, key_slice(c), :], qt)
        s_refs[slot][...] = st
        mx_ref[slot] = jnp.max(st, axis=0, keepdims=True)

    def consume(c, slot):
        s_ref = s_refs[slot]
        vt = vt_ref[0, 0, :, key_slice(c)]
        for j in range(tq // MXU_COLS):
            cs = slice(j * MXU_COLS, (j + 1) * MXU_COLS)
            m = m_ref[:, cs]
            m_new = jnp.maximum(m, mx_ref[slot, :, cs])
            p = jnp.exp2(s_ref[:, cs] - m_new).astype(BF)
            acc_ref[:, cs] = acc_ref[:, cs] * jnp.exp2(m - m_new) + _dot(vt, p)
            m_ref[:, cs] = m_new

    scores(0, 0)
    m_ref[...] = jnp.full(m_ref.shape, MASK_VALUE, F32)
    acc_ref[...] = jnp.zeros(acc_ref.shape, F32)

    def body(i, carry):
        c0 = unroll * i
        for u in range(unroll):
            scores(c0 + u + 1, (u + 1) % 2)
            consume(c0 + u, u % 2)
        return carry

    lax.fori_loop(0, nk // unroll - 1, body, 0)
    c0 = nk - unroll
    for u in range(unroll):
        if u + 1 < unroll:
            scores(c0 + u + 1, (u + 1) % 2)
        consume(c0 + u, u % 2)
    acc = acc_ref[...]
    o_ref[0, 0] = (acc[0:MLA_V] / acc[MLA_V:MLA_V + 1]).astype(BF)


def _mla_call(q, k, vt, tq, tk, unroll):
    b, h, s, _ = k.shape
    assert unroll % 2 == 0 and (s // tk) % unroll == 0
    return pl.pallas_call(
        functools.partial(_mla_kernel, tk=tk, unroll=unroll),
        grid=(b, h, s // tq),
        in_specs=[pl.BlockSpec((1, 1, HEAD_PAD, tq), lambda i, j, t: (i, j, 0, t)),
                  pl.BlockSpec((1, 1, s, HEAD_PAD), lambda i, j, t: (i, j, 0, 0)),
                  pl.BlockSpec((1, 1, VT_ROWS, s), lambda i, j, t: (i, j, 0, 0))],
        out_specs=pl.BlockSpec((1, 1, MLA_V, tq), lambda i, j, t: (i, j, 0, t)),
        out_shape=jax.ShapeDtypeStruct((b, h, MLA_V, s), BF),
        scratch_shapes=[pltpu.VMEM((tk, tq), F32), pltpu.VMEM((tk, tq), F32), pltpu.VMEM((2, 1, tq), F32),
                        pltpu.VMEM((VT_ROWS, tq), F32), pltpu.VMEM((1, tq), F32)],
        compiler_params=_params(("parallel", "parallel", "parallel")), name="mla_attn",
    )(q, k, vt)


def _merge_kernel(x_ref, ona_ref, omt_ref, gate_ref, wna_ref, wmla_ref, wout_ref, g_ref, b_ref,
                  o_ref, ob_ref, *, alpha):
    d = x_ref.shape[2]
    a = _dot(ona_ref[0], wna_ref[...])
    m = _dot_tn(omt_ref[0], wmla_ref[...])
    gate = gate_ref[0]
    merged = gate[:, :d].astype(F32) * a + gate[:, d:].astype(F32) * m
    y = alpha * x_ref[0] + _dot(merged.astype(BF), wout_ref[...])
    out = _layer_norm(y, g_ref[...], b_ref[...])
    o_ref[0] = out
    ob_ref[0] = out.astype(BF)


def _merge_call(x, o_na, o_mla_t, gates, w, alpha, tm):
    b, s, d = x.shape
    weights = [w['wnao'], w['wmlao'], w['wout'], w['ln1g'], w['ln1b']]
    in_specs = [pl.BlockSpec((1, tm, d), lambda i, j: (i, j, 0)),
                pl.BlockSpec((1, tm, NA_WIDTH), lambda i, j: (i, j, 0)),
                pl.BlockSpec((1, MLA_HEADS * MLA_V, tm), lambda i, j: (i, 0, j)),
                pl.BlockSpec((1, tm, 2 * d), lambda i, j: (i, j, 0))] + [_const_spec(a.shape) for a in weights]
    return pl.pallas_call(
        functools.partial(_merge_kernel, alpha=alpha),
        grid=(b, s // tm), in_specs=in_specs,
        out_specs=(pl.BlockSpec((1, tm, d), lambda i, j: (i, j, 0)),
                   pl.BlockSpec((1, tm, d), lambda i, j: (i, j, 0))),
        out_shape=(jax.ShapeDtypeStruct((b, s, d), F32), jax.ShapeDtypeStruct((b, s, d), BF)),
        compiler_params=_params(("parallel", "parallel")), name="merge_ln1",
    )(x, o_na, o_mla_t, gates, *weights)


def _ple_ln2(x, xb, f, p_ref, wpg_ref, wp_ref, g_ref, b_ref, alpha):
    ple = jax.nn.sigmoid(_dot(xb, wpg_ref[...])) * _dot(p_ref[0].astype(BF), wp_ref[...])
    return _layer_norm(alpha * x + f + ple, g_ref[...], b_ref[...])


def _ffn_kernel(x_ref, p_ref, w1_ref, w3_ref, w2_ref, wpg_ref, wp_ref, g_ref, b_ref, o_ref,
                *, alpha, chunks):
    x = x_ref[0]
    xb = x.astype(BF)
    fc = w1_ref.shape[1] // chunks
    f = None
    for c in range(chunks):
        sl = slice(c * fc, (c + 1) * fc)
        h = _silu(_dot(xb, w1_ref[:, sl])) * _dot(xb, w3_ref[:, sl])
        part = _dot(h.astype(BF), w2_ref[sl, :])
        f = part if f is None else f + part
    o_ref[0] = _ple_ln2(x, xb, f, p_ref, wpg_ref, wp_ref, g_ref, b_ref, alpha)


def _ffn_call(x, p, w, alpha, tm, chunks):
    b, s, d = x.shape
    weights = [w['w1'], w['w3'], w['w2'], w['wpg'], w['wp'], w['ln2g'], w['ln2b']]
    in_specs = [pl.BlockSpec((1, tm, d), lambda i, j: (i, j, 0)),
                pl.BlockSpec((1, tm, p.shape[-1]), lambda i, j: (i, j, 0))] + [_const_spec(a.shape) for a in weights]
    return pl.pallas_call(
        functools.partial(_ffn_kernel, alpha=alpha, chunks=chunks),
        grid=(b, s // tm), in_specs=in_specs,
        out_specs=pl.BlockSpec((1, tm, d), lambda i, j: (i, j, 0)),
        out_shape=jax.ShapeDtypeStruct((b, s, d), F32),
        compiler_params=_params(("parallel", "parallel")), name="ffn_ple_ln2",
    )(x, p, *weights)


def _router_kernel(x_ref, wh_ref, wl_ref, b_ref, o_ref):
    x = x_ref[...]
    xh = x.astype(BF)
    xl = (x - xh.astype(F32)).astype(BF)
    lg = _dot_nt(wh_ref[...], xh) + _dot_nt(wh_ref[...], xl) + _dot_nt(wl_ref[...], xh) + b_ref[...]
    rid = lax.broadcasted_iota(jnp.int32, lg.shape, 0)
    pad = lg.shape[0]
    m1 = jnp.max(lg, axis=0, keepdims=True)
    i1 = jnp.min(jnp.where(lg == m1, rid, pad), axis=0, keepdims=True)
    lg2 = jnp.where(rid == i1, MASK_VALUE, lg)
    m2 = jnp.max(lg2, axis=0, keepdims=True)
    i2 = jnp.min(jnp.where(lg2 == m2, rid, pad), axis=0, keepdims=True)
    e = jnp.exp(m2 - m1)
    o_ref[0:1, :] = i1.astype(F32)
    o_ref[1:2, :] = i2.astype(F32)
    o_ref[2:3, :] = 1.0 / (1.0 + e)
    o_ref[3:4, :] = e / (1.0 + e)
    o_ref[4:8, :] = jnp.zeros((4, lg.shape[1]), F32)


def _router_call(x2, wh, wl, bias, tm):
    n, d = x2.shape
    return pl.pallas_call(
        _router_kernel, grid=(n // tm,),
        in_specs=[pl.BlockSpec((tm, d), lambda i: (i, 0)), _const_spec(wh.shape), _const_spec(wl.shape),
                  _const_spec(bias.shape)],
        out_specs=pl.BlockSpec((8, tm), lambda i: (0, i)),
        out_shape=jax.ShapeDtypeStruct((8, n), F32),
        compiler_params=_params(("parallel",)), name="router",
    )(x2, wh, wl, bias)


def _experts_kernel(te_ref, na_ref, xs_ref, w1_ref, w3_ref, w2_ref, o_ref, acc_ref, *, chunks):
    t = pl.program_id(0)
    c = pl.program_id(1)
    active = t < na_ref[0]

    @pl.when(active)
    def _():
        xb = xs_ref[...]
        h = _silu(_dot(xb, w1_ref[0])) * _dot(xb, w3_ref[0])
        part = _dot(h.astype(BF), w2_ref[0])

        @pl.when(c == 0)
        def _():
            acc_ref[...] = part

        @pl.when(c > 0)
        def _():
            acc_ref[...] += part

        @pl.when(c == chunks - 1)
        def _():
            o_ref[...] = acc_ref[...].astype(BF)

    @pl.when(jnp.logical_and(jnp.logical_not(active), c == chunks - 1))
    def _():
        o_ref[...] = jnp.zeros_like(o_ref)


def _experts_call(tile_expert, n_active, xs, w1, w3, w2, tm, chunks):
    p_rows, d = xs.shape
    fc = w1.shape[2] // chunks
    grid_spec = pltpu.PrefetchScalarGridSpec(
        num_scalar_prefetch=2, grid=(p_rows // tm, chunks),
        in_specs=[pl.BlockSpec((tm, d), lambda t, c, te, na: (t, 0)),
                  pl.BlockSpec((1, d, fc), lambda t, c, te, na: (te[t], 0, c)),
                  pl.BlockSpec((1, d, fc), lambda t, c, te, na: (te[t], 0, c)),
                  pl.BlockSpec((1, fc, d), lambda t, c, te, na: (te[t], c, 0))],
        out_specs=pl.BlockSpec((tm, d), lambda t, c, te, na: (t, 0)),
        scratch_shapes=[pltpu.VMEM((tm, d), F32)])
    return pl.pallas_call(
        functools.partial(_experts_kernel, chunks=chunks), grid_spec=grid_spec,
        out_shape=jax.ShapeDtypeStruct((p_rows, d), BF),
        compiler_params=_params(("arbitrary", "arbitrary")), name="experts",
    )(tile_expert, n_active, xs, w1, w3, w2)


def _combine_kernel(x_ref, ya_ref, yb_ref, gw_ref, p_ref, wpg_ref, wp_ref, g_ref, b_ref, o_ref, *, alpha):
    x = x_ref[0]
    gw = gw_ref[0]
    f = gw[:, 0:1] * ya_ref[0].astype(F32) + gw[:, 1:2] * yb_ref[0].astype(F32)
    o_ref[0] = _ple_ln2(x, x.astype(BF), f, p_ref, wpg_ref, wp_ref, g_ref, b_ref, alpha)


def _combine_call(x, ya, yb, gw, p, w, alpha, tm):
    b, s, d = x.shape
    weights = [w['wpg'], w['wp'], w['ln2g'], w['ln2b']]
    tok = lambda width: pl.BlockSpec((1, tm, width), lambda i, j: (i, j, 0))
    in_specs = [tok(d), tok(d), tok(d), tok(2), tok(p.shape[-1])] + [_const_spec(a.shape) for a in weights]
    return pl.pallas_call(
        functools.partial(_combine_kernel, alpha=alpha),
        grid=(b, s // tm), in_specs=in_specs, out_specs=tok(d),
        out_shape=jax.ShapeDtypeStruct((b, s, d), F32),
        compiler_params=_params(("parallel", "parallel")), name="moe_combine_ple_ln2",
    )(x, ya, yb, gw, p, *weights)


def _moe_layer(x1, x1b, p, w, alpha, tl):
    b, s, d = x1.shape
    n = b * s
    tm = tl['moe_tm']
    route = _router_call(x1.reshape(n, d), w['wrh'], w['wrl'], w['br'], tl['tm'])
    e_all = route[0:2].astype(jnp.int32).reshape(2 * n)
    gw = route[2:4].T.reshape(b, s, 2)

    onehot = (e_all[:, None] == jnp.arange(N_EXPERTS, dtype=jnp.int32)[None, :]).astype(jnp.int32)
    csum = jnp.cumsum(onehot, axis=0)
    rank = jnp.sum(onehot * (csum - 1), axis=1)
    counts = csum[-1]
    padded = ((counts + tm - 1) // tm) * tm
    ends = jnp.cumsum(padded)
    starts = ends - padded
    dest = starts[e_all] + rank
    p_rows = 2 * n + N_EXPERTS * tm
    n_active = (ends[-1] // tm).astype(jnp.int32).reshape(1)
    tile_start = jnp.minimum(jnp.arange(p_rows // tm, dtype=jnp.int32), n_active[0] - 1) * tm
    tile_expert = jnp.sum((tile_start[:, None] >= ends[None, :]).astype(jnp.int32), axis=1)
    tile_expert = jnp.minimum(tile_expert, N_EXPERTS - 1).astype(jnp.int32)
    order = jnp.argsort(e_all, stable=True).astype(jnp.int32)
    row_expert = jnp.repeat(tile_expert, tm)
    row_rank = jnp.arange(p_rows, dtype=jnp.int32) - starts[row_expert]
    src = jnp.clip((jnp.cumsum(counts) - counts)[row_expert] + row_rank, 0, 2 * n - 1)
    row_token = jnp.where(row_rank < counts[row_expert], order[src] % n, 0)

    xs = jnp.take(x1b.reshape(n, d), row_token, axis=0)
    ys = _experts_call(tile_expert, n_active, xs, w['mw1'], w['mw3'], w['mw2'], tm, tl['moe_chunks'])
    ya = jnp.take(ys, dest[:n], axis=0).reshape(b, s, d)
    yb = jnp.take(ys, dest[n:], axis=0).reshape(b, s, d)
    return _combine_call(x1, ya, yb, gw, p, w, alpha, tl['tm'])


def _rope_tables(s):
    pos = jnp.arange(s, dtype=F32)
    inv_freq = ROPE_THETA ** (-jnp.arange(0, MLA_ROPE // 2, dtype=F32) * (2.0 / MLA_ROPE))
    ang = pos[:, None] * inv_freq[None, :]
    cos, sin = jnp.cos(ang), jnp.sin(ang)
    ck = jnp.concatenate([jnp.ones((s, MLA_NOPE), F32), cos, cos,
                          jnp.zeros((s, HEAD_PAD - MLA_NOPE - MLA_ROPE), F32)], axis=1)
    sk = jnp.concatenate([jnp.zeros((s, MLA_NOPE), F32), sin, sin,
                          jnp.zeros((s, HEAD_PAD - MLA_NOPE - MLA_ROPE), F32)], axis=1)
    qs = (MLA_NOPE + MLA_ROPE) ** -0.5 * math.log2(math.e)
    return (ck * qs).T, (sk * qs).T, ck, sk


def _rotate_half_cols(w_rope):
    half = w_rope.shape[-1] // 2
    return jnp.concatenate([-w_rope[..., half:], w_rope[..., :half]], axis=-1)


def _prep_layer(i, w_in, b_gate, q_norm_g, w_uq, kv_norm_g, w_ukv, w_na_o, w_mla_o, w_out, ln1_g, ln1_b,
                w_ple_gate, w_ple, ln2_g, ln2_b):
    d = w_in.shape[1]
    o = 0
    wna = w_in[i, :, o:o + 3 * NA_WIDTH]; o += 3 * NA_WIDTH
    wql = w_in[i, :, o:o + MLA_Q_LORA]; o += MLA_Q_LORA
    wkvl = w_in[i, :, o:o + MLA_KV_LORA]; o += MLA_KV_LORA
    wkr = w_in[i, :, o:o + MLA_ROPE]; o += MLA_ROPE
    wg = w_in[i, :, o:o + 2 * d]
    na_scale = jnp.concatenate([jnp.full((NA_WIDTH,), NA_HEAD_DIM ** -0.5, F32), jnp.ones((2 * NA_WIDTH,), F32)])
    wna = wna * na_scale[None, :]

    zpad = HEAD_PAD - MLA_NOPE - MLA_ROPE
    uq = w_uq[i].reshape(MLA_Q_LORA, MLA_HEADS, MLA_NOPE + MLA_ROPE)
    assert zpad == MLA_ROPE
    uq_pad = jnp.concatenate([uq, _rotate_half_cols(uq[..., MLA_NOPE:])], axis=-1)
    kr_pad = jnp.pad(wkr, ((0, 0), (MLA_NOPE, zpad)))
    kr_rot = jnp.pad(_rotate_half_cols(wkr), ((0, 0), (MLA_NOPE, zpad)))
    ukv = w_ukv[i].reshape(MLA_KV_LORA, MLA_HEADS, MLA_NOPE + MLA_V)
    uk_pad = jnp.pad(ukv[..., :MLA_NOPE], ((0, 0), (0, 0), (0, HEAD_PAD - MLA_NOPE)))
    uvt = ukv[..., MLA_NOPE:].reshape(MLA_KV_LORA, MLA_HEADS * MLA_V).T

    return dict(
        wna=wna.astype(BF), wg=wg.astype(BF), bg=b_gate[i][None, :], wql=wql.astype(BF), wkvl=wkvl.astype(BF),
        wkr=jnp.concatenate([kr_pad, kr_rot], axis=1).astype(BF),
        gq=q_norm_g[i][None, :], gkv=kv_norm_g[i][None, :],
        wuq=uq_pad.reshape(MLA_Q_LORA, MLA_HEADS * HEAD_PAD).T.astype(BF),
        wuk=uk_pad.reshape(MLA_KV_LORA, MLA_HEADS * HEAD_PAD).astype(BF), wuvt=uvt.astype(BF),
        wnao=w_na_o[i].astype(BF), wmlao=w_mla_o[i].astype(BF), wout=w_out[i].astype(BF),
        ln1g=ln1_g[i][None, :], ln1b=ln1_b[i][None, :],
        wpg=w_ple_gate[i].astype(BF), wp=w_ple[i].astype(BF), ln2g=ln2_g[i][None, :], ln2b=ln2_b[i][None, :],
    )


def kernel(x, p, w_in, b_gate, q_norm_g, w_uq, kv_norm_g, w_ukv, na_rpb, w_na_o, w_mla_o, w_out, ln1_g, ln1_b,
           ffn_w1, ffn_w3, ffn_w2, moe_w_router, moe_b_router, moe_w1, moe_w3, moe_w2, w_ple_gate, w_ple,
           ln2_g, ln2_b):
    b, s, d = x.shape
    depth = w_in.shape[0]
    alpha = (2 * depth) ** 0.25
    tl = _tiles(b * s, s)
    tabs = _rope_tables(s)

    for i in range(depth):
        w = _prep_layer(i, w_in, b_gate, q_norm_g, w_uq, kv_norm_g, w_ukv, w_na_o, w_mla_o, w_out, ln1_g, ln1_b,
                        w_ple_gate, w_ple, ln2_g, ln2_b)
        qkv, gates, q_mla, k_mla, vt = _proj_call(x, w, tabs, tl['tm'])
        o_na = _na_call(qkv, _na_bias_table(na_rpb[i]), tl['na_rows'])
        o_mla_t = _mla_call(q_mla, k_mla, vt, tl['tq'], tl['tk'], tl['mla_unroll'])
        o_mla_t = o_mla_t.reshape(b, MLA_HEADS * MLA_V, s)
        x1, x1b = _merge_call(x, o_na, o_mla_t, gates, w, alpha, tl['tm'])
        j = i // 2
        if i % 2 == 0:
            w.update(w1=ffn_w1[j].astype(BF), w3=ffn_w3[j].astype(BF), w2=ffn_w2[j].astype(BF))
            x = _ffn_call(x1, p[i], w, alpha, tl['tm'], tl['ffn_chunks'])
        else:
            wr = jnp.pad(moe_w_router[j].T, ((0, BF16_SUBLANES - N_EXPERTS), (0, 0)))
            wrh = wr.astype(BF)
            br = jnp.concatenate([moe_b_router[j], jnp.full((BF16_SUBLANES - N_EXPERTS,), MASK_VALUE, F32)])
            w.update(wrh=wrh, wrl=(wr - wrh.astype(F32)).astype(BF), br=br[:, None],
                     mw1=moe_w1[j].astype(BF), mw3=moe_w3[j].astype(BF), mw2=moe_w2[j].astype(BF))
            x = _moe_layer(x1, x1b, p[i], w, alpha, tl)
    return x
```

```python
import functools
import math

import numpy as np
import jax
import jax.numpy as jnp
from jax import lax
from jax.experimental import pallas as pl
from jax.experimental.pallas import tpu as pltpu

BF = jnp.bfloat16
F32 = jnp.float32

GRID_W = 64
NA_HEADS = 8
NA_HEAD_DIM = 64
NA_ROWS = 8
NA_COLS = 16
NA_WIDTH = NA_HEADS * NA_HEAD_DIM
MLA_HEADS = 8
MLA_NOPE = 64
MLA_ROPE = 32
MLA_V = 64
MLA_Q_LORA = 768
MLA_KV_LORA = 256
ROPE_THETA = 10000.0
N_EXPERTS = 8
LN_EPS = 1e-5
RMS_EPS = 1e-6
MASK_VALUE = -1e30

LANES = 128
BF16_SUBLANES = 16
MXU_COLS = 256
VMEM_LIMIT_BYTES = 56 * 1024 * 1024

HEAD_PAD = LANES
VT_ROWS = MLA_V + BF16_SUBLANES


def _dot(a, b):
    return jnp.dot(a, b, preferred_element_type=F32)


def _dot_nt(a, b):
    return lax.dot_general(a, b, (((1,), (1,)), ((), ())), preferred_element_type=F32)


def _dot_tn(a, b):
    return lax.dot_general(a, b, (((0,), (0,)), ((), ())), preferred_element_type=F32)


def _const_spec(shape):
    nd = len(shape)
    return pl.BlockSpec(shape, lambda *_: (0,) * nd, pipeline_mode=pl.Buffered(1))


def _params(sem):
    return pltpu.CompilerParams(dimension_semantics=sem, vmem_limit_bytes=VMEM_LIMIT_BYTES)


def _tiles(n_tokens, seq):
    return dict(
        tm=min(512, seq),
        tq=min(1024, seq),
        tk=min(256, seq),
        mla_unroll=8 if (seq // min(256, seq)) % 8 == 0 else 2,
        na_rows=8,
        moe_tm=512 if n_tokens >= 8192 else 128,
        moe_chunks=2,
        ffn_chunks=2,
    )


def _layer_norm(y, g, b):
    mu = jnp.mean(y, axis=-1, keepdims=True)
    yc = y - mu
    var = jnp.mean(yc * yc, axis=-1, keepdims=True)
    return yc * lax.rsqrt(var + LN_EPS) * g + b


def _rms_norm(y, g):
    return y * lax.rsqrt(jnp.mean(y * y, axis=-1, keepdims=True) + RMS_EPS) * g


def _silu(y):
    return y * jax.nn.sigmoid(y)


def _proj_kernel(x_ref, wna_ref, wg_ref, bg_ref, wql_ref, wkvl_ref, wkr_ref, gq_ref, gkv_ref,
                 wuq_ref, wuk_ref, wuvt_ref, cq_ref, sq_ref, ck_ref, sk_ref,
                 qkv_ref, gate_ref, q_ref, k_ref, vt_ref):
    xb = x_ref[0].astype(BF)
    qkv_ref[0] = _dot(xb, wna_ref[...]).astype(BF)
    gate_ref[0] = jax.nn.sigmoid(_dot(xb, wg_ref[...]) + bg_ref[...]).astype(BF)

    qn = _rms_norm(_dot(xb, wql_ref[...]), gq_ref[...]).astype(BF)
    qat = _dot_nt(wuq_ref[...], qn)
    cq, sq = cq_ref[...], sq_ref[...]
    for h in range(MLA_HEADS):
        qh = qat[h * HEAD_PAD:(h + 1) * HEAD_PAD]
        partner = pltpu.roll(qh, shift=HEAD_PAD - MLA_ROPE, axis=0)
        q_ref[0, h] = (qh * cq + partner * sq).astype(BF)

    kvn = _rms_norm(_dot(xb, wkvl_ref[...]), gkv_ref[...]).astype(BF)
    kr2 = _dot(xb, wkr_ref[...])
    kr = kr2[:, :HEAD_PAD] * ck_ref[...] + kr2[:, HEAD_PAD:] * sk_ref[...]
    kn = _dot(kvn, wuk_ref[...])
    for h in range(MLA_HEADS):
        sl = slice(h * HEAD_PAD, (h + 1) * HEAD_PAD)
        k_ref[0, h] = (kn[:, sl] + kr).astype(BF)

    vt = _dot_nt(wuvt_ref[...], kvn)
    ones = jnp.ones((BF16_SUBLANES, vt.shape[1]), BF)
    for h in range(MLA_HEADS):
        vt_ref[0, h, 0:MLA_V, :] = vt[h * MLA_V:(h + 1) * MLA_V].astype(BF)
        vt_ref[0, h, MLA_V:VT_ROWS, :] = ones


def _proj_call(x, w, tabs, tm):
    b, s, d = x.shape
    grid = (b, s // tm)
    weights = [w['wna'], w['wg'], w['bg'], w['wql'], w['wkvl'], w['wkr'], w['gq'], w['gkv'],
               w['wuq'], w['wuk'], w['wuvt']]
    tab_spec = pl.BlockSpec((tm, HEAD_PAD), lambda i, j: (j, 0))
    tab_t_spec = pl.BlockSpec((HEAD_PAD, tm), lambda i, j: (0, j))
    in_specs = ([pl.BlockSpec((1, tm, d), lambda i, j: (i, j, 0))]
                + [_const_spec(a.shape) for a in weights] + [tab_t_spec] * 2 + [tab_spec] * 2)
    out_shape = (
        jax.ShapeDtypeStruct((b, s, 3 * NA_WIDTH), BF),
        jax.ShapeDtypeStruct((b, s, 2 * d), BF),
        jax.ShapeDtypeStruct((b, MLA_HEADS, HEAD_PAD, s), BF),
        jax.ShapeDtypeStruct((b, MLA_HEADS, s, HEAD_PAD), BF),
        jax.ShapeDtypeStruct((b, MLA_HEADS, VT_ROWS, s), BF),
    )
    out_specs = (
        pl.BlockSpec((1, tm, 3 * NA_WIDTH), lambda i, j: (i, j, 0)),
        pl.BlockSpec((1, tm, 2 * d), lambda i, j: (i, j, 0)),
        pl.BlockSpec((1, MLA_HEADS, HEAD_PAD, tm), lambda i, j: (i, 0, 0, j)),
        pl.BlockSpec((1, MLA_HEADS, tm, HEAD_PAD), lambda i, j: (i, 0, j, 0)),
        pl.BlockSpec((1, MLA_HEADS, VT_ROWS, tm), lambda i, j: (i, 0, 0, j)),
    )
    return pl.pallas_call(
        _proj_kernel, grid=grid, in_specs=in_specs, out_specs=out_specs, out_shape=out_shape,
        compiler_params=_params(("parallel", "parallel")), name="proj",
    )(x, *weights, *tabs)


def _na_kernel(q_ref, kp_ref, kc_ref, kn_ref, vp_ref, vc_ref, vn_ref, bias_ref, o_ref,
               kcat_ref, vcat_ref, *, rows, rb):
    i = pl.program_id(1)
    blk = rb * GRID_W
    win = NA_ROWS * GRID_W
    kcat_ref[0:blk] = kp_ref[0]
    kcat_ref[blk:2 * blk] = kc_ref[0]
    kcat_ref[2 * blk:3 * blk] = kn_ref[0]
    vcat_ref[0:blk] = vp_ref[0]
    vcat_ref[blk:2 * blk] = vc_ref[0]
    vcat_ref[2 * blk:3 * blk] = vn_ref[0]
    lane = lax.broadcasted_iota(jnp.int32, (GRID_W, LANES), 1)
    first_head = lane < NA_HEAD_DIM

    def row(a, carry):
        r = i * rb + a
        rs = jnp.clip(r - NA_ROWS // 2, 0, rows - NA_ROWS)
        off = pl.multiple_of((rs - (i - 1) * rb) * GRID_W, GRID_W)
        d = rs - r + (NA_ROWS - 1)
        qoff = pl.multiple_of(a * GRID_W, GRID_W)
        qrow = q_ref[0, pl.ds(qoff, GRID_W), :]
        zero = jnp.zeros((GRID_W, LANES), BF)
        sc = []
        for hp in range(NA_HEADS // 2):
            sl = slice(hp * LANES, (hp + 1) * LANES)
            qp = qrow[:, sl]
            qm = jnp.concatenate([jnp.where(first_head, qp, zero), jnp.where(first_head, zero, qp)], axis=0)
            sc.append(_dot_nt(qm, kcat_ref[pl.ds(off, win), sl]))
        sc = jnp.concatenate(sc, axis=0) + bias_ref[d]
        m = jnp.max(sc, axis=-1, keepdims=True)
        p = jnp.exp(sc - m)
        inv_l = 1.0 / jnp.sum(p, axis=-1, keepdims=True)
        pb = p.astype(BF)
        for hp in range(NA_HEADS // 2):
            sl = slice(hp * LANES, (hp + 1) * LANES)
            rows2 = slice(hp * 2 * GRID_W, (hp + 1) * 2 * GRID_W)
            o2 = _dot(pb[rows2], vcat_ref[pl.ds(off, win), sl]) * inv_l[rows2]
            o_ref[0, pl.ds(qoff, GRID_W), sl] = jnp.where(first_head, o2[:GRID_W], o2[GRID_W:]).astype(BF)
        return carry

    lax.fori_loop(0, rb, row, 0, unroll=2)


def _na_call(qkv, bias_tab, rb):
    b, s, _ = qkv.shape
    rows = s // GRID_W
    nb = rows // rb
    blk = rb * GRID_W

    def kv_spec(col, shift):
        return pl.BlockSpec((1, blk, NA_WIDTH),
                            lambda i, j: (i, jnp.clip(j + shift, 0, nb - 1), col))

    in_specs = [pl.BlockSpec((1, blk, NA_WIDTH), lambda i, j: (i, j, 0)),
                kv_spec(1, -1), kv_spec(1, 0), kv_spec(1, 1),
                kv_spec(2, -1), kv_spec(2, 0), kv_spec(2, 1),
                _const_spec(bias_tab.shape)]
    return pl.pallas_call(
        functools.partial(_na_kernel, rows=rows, rb=rb),
        grid=(b, nb), in_specs=in_specs,
        out_specs=pl.BlockSpec((1, blk, NA_WIDTH), lambda i, j: (i, j, 0)),
        out_shape=jax.ShapeDtypeStruct((b, s, NA_WIDTH), BF),
        scratch_shapes=[pltpu.VMEM((3 * blk, NA_WIDTH), BF), pltpu.VMEM((3 * blk, NA_WIDTH), BF)],
        compiler_params=_params(("parallel", "parallel")), name="na_attn",
    )(qkv, qkv, qkv, qkv, qkv, qkv, qkv, bias_tab)


def _na_bias_table(rpb):
    qc = np.arange(GRID_W)[:, None]
    kc = np.arange(GRID_W)[None, :]
    win_start = np.clip(qc - NA_COLS // 2, 0, GRID_W - NA_COLS)
    valid = (kc >= win_start) & (kc < win_start + NA_COLS)
    dc = np.clip(kc - qc + NA_COLS - 1, 0, 2 * NA_COLS - 2)
    tab = jnp.where(jnp.asarray(valid)[None, None], rpb[:, :, dc], MASK_VALUE)
    t = jnp.stack([tab[:, d:d + NA_ROWS] for d in range(NA_ROWS)], axis=0)
    t = t.transpose(0, 1, 3, 2, 4)
    return t.reshape(NA_ROWS, NA_HEADS * GRID_W, NA_ROWS * GRID_W).astype(F32)


def _mla_kernel(q_ref, k_ref, vt_ref, o_ref, s0_ref, s1_ref, mx_ref, acc_ref, m_ref, *, tk, unroll):
    qt = q_ref[0, 0]
    tq = qt.shape[1]
    nk = k_ref.shape[2] // tk
    s_refs = (s0_ref, s1_ref)

    def key_slice(c):
        return pl.ds(pl.multiple_of(c * tk, tk), tk)

    def scores(c, slot):
        st = _dot(k_ref[0, 0, key_slice(c), :], qt)
        s_refs[slot][...] = st
        mx_ref[slot] = jnp.max(st, axis=0, keepdims=True)

    def consume(c, slot):
        s_ref = s_refs[slot]
        vt = vt_ref[0, 0, :, key_slice(c)]
        for j in range(tq // MXU_COLS):
            cs = slice(j * MXU_COLS, (j + 1) * MXU_COLS)
            m = m_ref[:, cs]
            m_new = jnp.maximum(m, mx_ref[slot, :, cs])
            p = jnp.exp2(s_ref[:, cs] - m_new).astype(BF)
            acc_ref[:, cs] = acc_ref[:, cs] * jnp.exp2(m - m_new) + _dot(vt, p)
            m_ref[:, cs] = m_new

    scores(0, 0)
    m_ref[...] = jnp.full(m_ref.shape, MASK_VALUE, F32)
    acc_ref[...] = jnp.zeros(acc_ref.shape, F32)

    def body(i, carry):
        c0 = unroll * i
        for u in range(unroll):
            scores(c0 + u + 1, (u + 1) % 2)
            consume(c0 + u, u % 2)
        return carry

    lax.fori_loop(0, nk // unroll - 1, body, 0)
    c0 = nk - unroll
    for u in range(unroll):
        if u + 1 < unroll:
            scores(c0 + u + 1, (u + 1) % 2)
        consume(c0 + u, u % 2)
    acc = acc_ref[...]
    o_ref[0, 0] = (acc[0:MLA_V] / acc[MLA_V:MLA_V + 1]).astype(BF)


def _mla_call(q, k, vt, tq, tk, unroll):
    b, h, s, _ = k.shape
    assert unroll % 2 == 0 and (s // tk) % unroll == 0
    return pl.pallas_call(
        functools.partial(_mla_kernel, tk=tk, unroll=unroll),
        grid=(b, h, s // tq),
        in_specs=[pl.BlockSpec((1, 1, HEAD_PAD, tq), lambda i, j, t: (i, j, 0, t)),
                  pl.BlockSpec((1, 1, s, HEAD_PAD), lambda i, j, t: (i, j, 0, 0)),
                  pl.BlockSpec((1, 1, VT_ROWS, s), lambda i, j, t: (i, j, 0, 0))],
        out_specs=pl.BlockSpec((1, 1, MLA_V, tq), lambda i, j, t: (i, j, 0, t)),
        out_shape=jax.ShapeDtypeStruct((b, h, MLA_V, s), BF),
        scratch_shapes=[pltpu.VMEM((tk, tq), F32), pltpu.VMEM((tk, tq), F32), pltpu.VMEM((2, 1, tq), F32),
                        pltpu.VMEM((VT_ROWS, tq), F32), pltpu.VMEM((1, tq), F32)],
        compiler_params=_params(("parallel", "parallel", "parallel")), name="mla_attn",
    )(q, k, vt)


def _merge_kernel(x_ref, ona_ref, omt_ref, gate_ref, wna_ref, wmla_ref, wout_ref, g_ref, b_ref,
                  o_ref, ob_ref, *, alpha):
    d = x_ref.shape[2]
    a = _dot(ona_ref[0], wna_ref[...])
    m = _dot_tn(omt_ref[0], wmla_ref[...])
    gate = gate_ref[0]
    merged = gate[:, :d].astype(F32) * a + gate[:, d:].astype(F32) * m
    y = alpha * x_ref[0] + _dot(merged.astype(BF), wout_ref[...])
    out = _layer_norm(y, g_ref[...], b_ref[...])
    o_ref[0] = out
    ob_ref[0] = out.astype(BF)


def _merge_call(x, o_na, o_mla_t, gates, w, alpha, tm):
    b, s, d = x.shape
    weights = [w['wnao'], w['wmlao'], w['wout'], w['ln1g'], w['ln1b']]
    in_specs = [pl.BlockSpec((1, tm, d), lambda i, j: (i, j, 0)),
                pl.BlockSpec((1, tm, NA_WIDTH), lambda i, j: (i, j, 0)),
                pl.BlockSpec((1, MLA_HEADS * MLA_V, tm), lambda i, j: (i, 0, j)),
                pl.BlockSpec((1, tm, 2 * d), lambda i, j: (i, j, 0))] + [_const_spec(a.shape) for a in weights]
    return pl.pallas_call(
        functools.partial(_merge_kernel, alpha=alpha),
        grid=(b, s // tm), in_specs=in_specs,
        out_specs=(pl.BlockSpec((1, tm, d), lambda i, j: (i, j, 0)),
                   pl.BlockSpec((1, tm, d), lambda i, j: (i, j, 0))),
        out_shape=(jax.ShapeDtypeStruct((b, s, d), F32), jax.ShapeDtypeStruct((b, s, d), BF)),
        compiler_params=_params(("parallel", "parallel")), name="merge_ln1",
    )(x, o_na, o_mla_t, gates, *weights)


def _ple_ln2(x, xb, f, p_ref, wpg_ref, wp_ref, g_ref, b_ref, alpha):
    ple = jax.nn.sigmoid(_dot(xb, wpg_ref[...])) * _dot(p_ref[0].astype(BF), wp_ref[...])
    return _layer_norm(alpha * x + f + ple, g_ref[...], b_ref[...])


def _ffn_kernel(x_ref, p_ref, w1_ref, w3_ref, w2_ref, wpg_ref, wp_ref, g_ref, b_ref, o_ref,
                *, alpha, chunks):
    x = x_ref[0]
    xb = x.astype(BF)
    fc = w1_ref.shape[1] // chunks
    f = None
    for c in range(chunks):
        sl = slice(c * fc, (c + 1) * fc)
        h = _silu(_dot(xb, w1_ref[:, sl])) * _dot(xb, w3_ref[:, sl])
        part = _dot(h.astype(BF), w2_ref[sl, :])
        f = part if f is None else f + part
    o_ref[0] = _ple_ln2(x, xb, f, p_ref, wpg_ref, wp_ref, g_ref, b_ref, alpha)


def _ffn_call(x, p, w, alpha, tm, chunks):
    b, s, d = x.shape
    weights = [w['w1'], w['w3'], w['w2'], w['wpg'], w['wp'], w['ln2g'], w['ln2b']]
    in_specs = [pl.BlockSpec((1, tm, d), lambda i, j: (i, j, 0)),
                pl.BlockSpec((1, tm, p.shape[-1]), lambda i, j: (i, j, 0))] + [_const_spec(a.shape) for a in weights]
    return pl.pallas_call(
        functools.partial(_ffn_kernel, alpha=alpha, chunks=chunks),
        grid=(b, s // tm), in_specs=in_specs,
        out_specs=pl.BlockSpec((1, tm, d), lambda i, j: (i, j, 0)),
        out_shape=jax.ShapeDtypeStruct((b, s, d), F32),
        compiler_params=_params(("parallel", "parallel")), name="ffn_ple_ln2",
    )(x, p, *weights)


def _router_kernel(x_ref, wh_ref, wl_ref, b_ref, o_ref):
    x = x_ref[...]
    xh = x.astype(BF)
    xl = (x - xh.astype(F32)).astype(BF)
    lg = _dot_nt(wh_ref[...], xh) + _dot_nt(wh_ref[...], xl) + _dot_nt(wl_ref[...], xh) + b_ref[...]
    rid = lax.broadcasted_iota(jnp.int32, lg.shape, 0)
    pad = lg.shape[0]
    m1 = jnp.max(lg, axis=0, keepdims=True)
    i1 = jnp.min(jnp.where(lg == m1, rid, pad), axis=0, keepdims=True)
    lg2 = jnp.where(rid == i1, MASK_VALUE, lg)
    m2 = jnp.max(lg2, axis=0, keepdims=True)
    i2 = jnp.min(jnp.where(lg2 == m2, rid, pad), axis=0, keepdims=True)
    e = jnp.exp(m2 - m1)
    o_ref[0:1, :] = i1.astype(F32)
    o_ref[1:2, :] = i2.astype(F32)
    o_ref[2:3, :] = 1.0 / (1.0 + e)
    o_ref[3:4, :] = e / (1.0 + e)
    o_ref[4:8, :] = jnp.zeros((4, lg.shape[1]), F32)


def _router_call(x2, wh, wl, bias, tm):
    n, d = x2.shape
    return pl.pallas_call(
        _router_kernel, grid=(n // tm,),
        in_specs=[pl.BlockSpec((tm, d), lambda i: (i, 0)), _const_spec(wh.shape), _const_spec(wl.shape),
                  _const_spec(bias.shape)],
        out_specs=pl.BlockSpec((8, tm), lambda i: (0, i)),
        out_shape=jax.ShapeDtypeStruct((8, n), F32),
        compiler_params=_params(("parallel",)), name="router",
    )(x2, wh, wl, bias)


def _experts_kernel(te_ref, na_ref, xs_ref, w1_ref, w3_ref, w2_ref, o_ref, *, chunks):
    active = pl.program_id(0) < na_ref[0]

    @pl.when(active)
    def _():
        xb = xs_ref[...]
        fc = w1_ref.shape[2] // chunks
        acc = None
        for c in range(chunks):
            sl = slice(c * fc, (c + 1) * fc)
            h = _silu(_dot(xb, w1_ref[0, :, sl])) * _dot(xb, w3_ref[0, :, sl])
            part = _dot(h.astype(BF), w2_ref[0, sl, :])
            acc = part if acc is None else acc + part
        o_ref[...] = acc.astype(BF)

    @pl.when(jnp.logical_not(active))
    def _():
        o_ref[...] = jnp.zeros_like(o_ref)


def _experts_call(tile_expert, n_active, xs, w1, w3, w2, tm, chunks):
    p_rows, d = xs.shape
    dff = w1.shape[2]

    def expert_spec(shape):
        return pl.BlockSpec(shape, lambda t, te, na: (te[t], 0, 0), pipeline_mode=pl.Buffered(1))

    grid_spec = pltpu.PrefetchScalarGridSpec(
        num_scalar_prefetch=2, grid=(p_rows // tm,),
        in_specs=[pl.BlockSpec((tm, d), lambda t, te, na: (t, 0)),
                  expert_spec((1, d, dff)), expert_spec((1, d, dff)), expert_spec((1, dff, d))],
        out_specs=pl.BlockSpec((tm, d), lambda t, te, na: (t, 0)))
    return pl.pallas_call(
        functools.partial(_experts_kernel, chunks=chunks), grid_spec=grid_spec,
        out_shape=jax.ShapeDtypeStruct((p_rows, d), BF),
        compiler_params=_params(("arbitrary",)), name="experts",
    )(tile_expert, n_active, xs, w1, w3, w2)


def _combine_kernel(x_ref, ya_ref, yb_ref, gw_ref, p_ref, wpg_ref, wp_ref, g_ref, b_ref, o_ref, *, alpha):
    x = x_ref[0]
    gw = gw_ref[0]
    f = gw[:, 0:1] * ya_ref[0].astype(F32) + gw[:, 1:2] * yb_ref[0].astype(F32)
    o_ref[0] = _ple_ln2(x, x.astype(BF), f, p_ref, wpg_ref, wp_ref, g_ref, b_ref, alpha)


def _combine_call(x, ya, yb, gw, p, w, alpha, tm):
    b, s, d = x.shape
    weights = [w['wpg'], w['wp'], w['ln2g'], w['ln2b']]
    tok = lambda width: pl.BlockSpec((1, tm, width), lambda i, j: (i, j, 0))
    in_specs = [tok(d), tok(d), tok(d), tok(2), tok(p.shape[-1])] + [_const_spec(a.shape) for a in weights]
    return pl.pallas_call(
        functools.partial(_combine_kernel, alpha=alpha),
        grid=(b, s // tm), in_specs=in_specs, out_specs=tok(d),
        out_shape=jax.ShapeDtypeStruct((b, s, d), F32),
        compiler_params=_params(("parallel", "parallel")), name="moe_combine_ple_ln2",
    )(x, ya, yb, gw, p, *weights)


def _moe_layer(x1, x1b, p, w, alpha, tl):
    b, s, d = x1.shape
    n = b * s
    tm = tl['moe_tm']
    route = _router_call(x1.reshape(n, d), w['wrh'], w['wrl'], w['br'], tl['tm'])
    e_all = route[0:2].astype(jnp.int32).reshape(2 * n)
    gw = route[2:4].T.reshape(b, s, 2)

    onehot = (e_all[:, None] == jnp.arange(N_EXPERTS, dtype=jnp.int32)[None, :]).astype(jnp.int32)
    csum = jnp.cumsum(onehot, axis=0)
    rank = jnp.sum(onehot * (csum - 1), axis=1)
    counts = csum[-1]
    padded = ((counts + tm - 1) // tm) * tm
    ends = jnp.cumsum(padded)
    starts = ends - padded
    dest = starts[e_all] + rank
    p_rows = 2 * n + N_EXPERTS * tm
    n_active = (ends[-1] // tm).astype(jnp.int32).reshape(1)
    tile_start = jnp.minimum(jnp.arange(p_rows // tm, dtype=jnp.int32), n_active[0] - 1) * tm
    tile_expert = jnp.sum((tile_start[:, None] >= ends[None, :]).astype(jnp.int32), axis=1)
    tile_expert = jnp.minimum(tile_expert, N_EXPERTS - 1).astype(jnp.int32)
    order = jnp.argsort(e_all, stable=True).astype(jnp.int32)
    row_expert = jnp.repeat(tile_expert, tm)
    row_rank = jnp.arange(p_rows, dtype=jnp.int32) - starts[row_expert]
    src = jnp.clip((jnp.cumsum(counts) - counts)[row_expert] + row_rank, 0, 2 * n - 1)
    row_token = jnp.where(row_rank < counts[row_expert], order[src] % n, 0)

    xs = x1b.reshape(n, d).at[row_token].get(mode='promise_in_bounds')
    ys = _experts_call(tile_expert, n_active, xs, w['mw1'], w['mw3'], w['mw2'], tm, tl['moe_chunks'])
    ya = ys.at[dest[:n]].get(mode='promise_in_bounds').reshape(b, s, d)
    yb = ys.at[dest[n:]].get(mode='promise_in_bounds').reshape(b, s, d)
    return _combine_call(x1, ya, yb, gw, p, w, alpha, tl['tm'])


def _rope_tables(s):
    pos = jnp.arange(s, dtype=F32)
    inv_freq = ROPE_THETA ** (-jnp.arange(0, MLA_ROPE // 2, dtype=F32) * (2.0 / MLA_ROPE))
    ang = pos[:, None] * inv_freq[None, :]
    cos, sin = jnp.cos(ang), jnp.sin(ang)
    ck = jnp.concatenate([jnp.ones((s, MLA_NOPE), F32), cos, cos,
                          jnp.zeros((s, HEAD_PAD - MLA_NOPE - MLA_ROPE), F32)], axis=1)
    sk = jnp.concatenate([jnp.zeros((s, MLA_NOPE), F32), sin, sin,
                          jnp.zeros((s, HEAD_PAD - MLA_NOPE - MLA_ROPE), F32)], axis=1)
    qs = (MLA_NOPE + MLA_ROPE) ** -0.5 * math.log2(math.e)
    return (ck * qs).T, (sk * qs).T, ck, sk


def _rotate_half_cols(w_rope):
    half = w_rope.shape[-1] // 2
    return jnp.concatenate([-w_rope[..., half:], w_rope[..., :half]], axis=-1)


def _prep_layer(i, w_in, b_gate, q_norm_g, w_uq, kv_norm_g, w_ukv, w_na_o, w_mla_o, w_out, ln1_g, ln1_b,
                w_ple_gate, w_ple, ln2_g, ln2_b):
    d = w_in.shape[1]
    o = 0
    wna = w_in[i, :, o:o + 3 * NA_WIDTH]; o += 3 * NA_WIDTH
    wql = w_in[i, :, o:o + MLA_Q_LORA]; o += MLA_Q_LORA
    wkvl = w_in[i, :, o:o + MLA_KV_LORA]; o += MLA_KV_LORA
    wkr = w_in[i, :, o:o + MLA_ROPE]; o += MLA_ROPE
    wg = w_in[i, :, o:o + 2 * d]
    na_scale = jnp.concatenate([jnp.full((NA_WIDTH,), NA_HEAD_DIM ** -0.5, F32), jnp.ones((2 * NA_WIDTH,), F32)])
    wna = wna * na_scale[None, :]

    zpad = HEAD_PAD - MLA_NOPE - MLA_ROPE
    uq = w_uq[i].reshape(MLA_Q_LORA, MLA_HEADS, MLA_NOPE + MLA_ROPE)
    assert zpad == MLA_ROPE
    uq_pad = jnp.concatenate([uq, _rotate_half_cols(uq[..., MLA_NOPE:])], axis=-1)
    kr_pad = jnp.pad(wkr, ((0, 0), (MLA_NOPE, zpad)))
    kr_rot = jnp.pad(_rotate_half_cols(wkr), ((0, 0), (MLA_NOPE, zpad)))
    ukv = w_ukv[i].reshape(MLA_KV_LORA, MLA_HEADS, MLA_NOPE + MLA_V)
    uk_pad = jnp.pad(ukv[..., :MLA_NOPE], ((0, 0), (0, 0), (0, HEAD_PAD - MLA_NOPE)))
    uvt = ukv[..., MLA_NOPE:].reshape(MLA_KV_LORA, MLA_HEADS * MLA_V).T

    return dict(
        wna=wna.astype(BF), wg=wg.astype(BF), bg=b_gate[i][None, :], wql=wql.astype(BF), wkvl=wkvl.astype(BF),
        wkr=jnp.concatenate([kr_pad, kr_rot], axis=1).astype(BF),
        gq=q_norm_g[i][None, :], gkv=kv_norm_g[i][None, :],
        wuq=uq_pad.reshape(MLA_Q_LORA, MLA_HEADS * HEAD_PAD).T.astype(BF),
        wuk=uk_pad.reshape(MLA_KV_LORA, MLA_HEADS * HEAD_PAD).astype(BF), wuvt=uvt.astype(BF),
        wnao=w_na_o[i].astype(BF), wmlao=w_mla_o[i].astype(BF), wout=w_out[i].astype(BF),
        ln1g=ln1_g[i][None, :], ln1b=ln1_b[i][None, :],
        wpg=w_ple_gate[i].astype(BF), wp=w_ple[i].astype(BF), ln2g=ln2_g[i][None, :], ln2b=ln2_b[i][None, :],
    )


def kernel(x, p, w_in, b_gate, q_norm_g, w_uq, kv_norm_g, w_ukv, na_rpb, w_na_o, w_mla_o, w_out, ln1_g, ln1_b,
           ffn_w1, ffn_w3, ffn_w2, moe_w_router, moe_b_router, moe_w1, moe_w3, moe_w2, w_ple_gate, w_ple,
           ln2_g, ln2_b):
    b, s, d = x.shape
    depth = w_in.shape[0]
    alpha = (2 * depth) ** 0.25
    tl = _tiles(b * s, s)
    tabs = _rope_tables(s)

    for i in range(depth):
        w = _prep_layer(i, w_in, b_gate, q_norm_g, w_uq, kv_norm_g, w_ukv, w_na_o, w_mla_o, w_out, ln1_g, ln1_b,
                        w_ple_gate, w_ple, ln2_g, ln2_b)
        qkv, gates, q_mla, k_mla, vt = _proj_call(x, w, tabs, tl['tm'])
        o_na = _na_call(qkv, _na_bias_table(na_rpb[i]), tl['na_rows'])
        o_mla_t = _mla_call(q_mla, k_mla, vt, tl['tq'], tl['tk'], tl['mla_unroll'])
        o_mla_t = o_mla_t.reshape(b, MLA_HEADS * MLA_V, s)
        x1, x1b = _merge_call(x, o_na, o_mla_t, gates, w, alpha, tl['tm'])
        j = i // 2
        if i % 2 == 0:
            w.update(w1=ffn_w1[j].astype(BF), w3=ffn_w3[j].astype(BF), w2=ffn_w2[j].astype(BF))
            x = _ffn_call(x1, p[i], w, alpha, tl['tm'], tl['ffn_chunks'])
        else:
            wr = jnp.pad(moe_w_router[j].T, ((0, BF16_SUBLANES - N_EXPERTS), (0, 0)))
            wrh = wr.astype(BF)
            br = jnp.concatenate([moe_b_router[j], jnp.full((BF16_SUBLANES - N_EXPERTS,), MASK_VALUE, F32)])
            w.update(wrh=wrh, wrl=(wr - wrh.astype(F32)).astype(BF), br=br[:, None],
                     mw1=moe_w1[j].astype(BF), mw3=moe_w3[j].astype(BF), mw2=moe_w2[j].astype(BF))
            x = _moe_layer(x1, x1b, p[i], w, alpha, tl)
    return x
```

```python
import functools
import math

import numpy as np
import jax
import jax.numpy as jnp
from jax import lax
from jax.experimental import pallas as pl
from jax.experimental.pallas import tpu as pltpu

BF = jnp.bfloat16
F32 = jnp.float32

GRID_W = 64
NA_HEADS = 8
NA_HEAD_DIM = 64
NA_ROWS = 8
NA_COLS = 16
NA_WIDTH = NA_HEADS * NA_HEAD_DIM
MLA_HEADS = 8
MLA_NOPE = 64
MLA_ROPE = 32
MLA_V = 64
MLA_Q_LORA = 768
MLA_KV_LORA = 256
ROPE_THETA = 10000.0
N_EXPERTS = 8
LN_EPS = 1e-5
RMS_EPS = 1e-6
MASK_VALUE = -1e30

LANES = 128
BF16_SUBLANES = 16
MXU_COLS = 256
VMEM_LIMIT_BYTES = 56 * 1024 * 1024

HEAD_PAD = LANES
VT_ROWS = MLA_V + BF16_SUBLANES


def _dot(a, b):
    return jnp.dot(a, b, preferred_element_type=F32)


def _dot_nt(a, b):
    return lax.dot_general(a, b, (((1,), (1,)), ((), ())), preferred_element_type=F32)


def _dot_tn(a, b):
    return lax.dot_general(a, b, (((0,), (0,)), ((), ())), preferred_element_type=F32)


def _const_spec(shape):
    nd = len(shape)
    return pl.BlockSpec(shape, lambda *_: (0,) * nd, pipeline_mode=pl.Buffered(1))


def _params(sem):
    return pltpu.CompilerParams(dimension_semantics=sem, vmem_limit_bytes=VMEM_LIMIT_BYTES)


def _tiles(n_tokens, seq):
    return dict(
        tm=min(512, seq),
        tq=min(1024, seq),
        tk=min(512, seq),
        mla_unroll=4 if (seq // min(512, seq)) % 4 == 0 else 2,
        na_rows=8,
        moe_tm=512 if n_tokens >= 8192 else 128,
        moe_chunks=2,
        ffn_chunks=2,
    )


def _layer_norm(y, g, b):
    mu = jnp.mean(y, axis=-1, keepdims=True)
    yc = y - mu
    var = jnp.mean(yc * yc, axis=-1, keepdims=True)
    return yc * lax.rsqrt(var + LN_EPS) * g + b


def _rms_norm(y, g):
    return y * lax.rsqrt(jnp.mean(y * y, axis=-1, keepdims=True) + RMS_EPS) * g


def _silu(y):
    return y * jax.nn.sigmoid(y)


def _proj_kernel(x_ref, wna_ref, wg_ref, bg_ref, wql_ref, wkvl_ref, wkr_ref, gq_ref, gkv_ref,
                 wuq_ref, wuk_ref, wuvt_ref, cq_ref, sq_ref, ck_ref, sk_ref,
                 qkv_ref, gate_ref, q_ref, k_ref, vt_ref):
    xb = x_ref[0].astype(BF)
    qkv_ref[0] = _dot(xb, wna_ref[...]).astype(BF)
    gate_ref[0] = jax.nn.sigmoid(_dot(xb, wg_ref[...]) + bg_ref[...]).astype(BF)

    qn = _rms_norm(_dot(xb, wql_ref[...]), gq_ref[...]).astype(BF)
    qat = _dot_nt(wuq_ref[...], qn)
    cq, sq = cq_ref[...], sq_ref[...]
    for h in range(MLA_HEADS):
        qh = qat[h * HEAD_PAD:(h + 1) * HEAD_PAD]
        partner = pltpu.roll(qh, shift=HEAD_PAD - MLA_ROPE, axis=0)
        q_ref[0, h] = (qh * cq + partner * sq).astype(BF)

    kvn = _rms_norm(_dot(xb, wkvl_ref[...]), gkv_ref[...]).astype(BF)
    kr2 = _dot(xb, wkr_ref[...])
    kr = kr2[:, :HEAD_PAD] * ck_ref[...] + kr2[:, HEAD_PAD:] * sk_ref[...]
    kn = _dot(kvn, wuk_ref[...])
    for h in range(MLA_HEADS):
        sl = slice(h * HEAD_PAD, (h + 1) * HEAD_PAD)
        k_ref[0, h] = (kn[:, sl] + kr).astype(BF)

    vt = _dot_nt(wuvt_ref[...], kvn)
    ones = jnp.ones((BF16_SUBLANES, vt.shape[1]), BF)
    for h in range(MLA_HEADS):
        vt_ref[0, h, 0:MLA_V, :] = vt[h * MLA_V:(h + 1) * MLA_V].astype(BF)
        vt_ref[0, h, MLA_V:VT_ROWS, :] = ones


def _proj_call(x, w, tabs, tm):
    b, s, d = x.shape
    grid = (b, s // tm)
    weights = [w['wna'], w['wg'], w['bg'], w['wql'], w['wkvl'], w['wkr'], w['gq'], w['gkv'],
               w['wuq'], w['wuk'], w['wuvt']]
    tab_spec = pl.BlockSpec((tm, HEAD_PAD), lambda i, j: (j, 0))
    tab_t_spec = pl.BlockSpec((HEAD_PAD, tm), lambda i, j: (0, j))
    in_specs = ([pl.BlockSpec((1, tm, d), lambda i, j: (i, j, 0))]
                + [_const_spec(a.shape) for a in weights] + [tab_t_spec] * 2 + [tab_spec] * 2)
    out_shape = (
        jax.ShapeDtypeStruct((b, s, 3 * NA_WIDTH), BF),
        jax.ShapeDtypeStruct((b, s, 2 * d), BF),
        jax.ShapeDtypeStruct((b, MLA_HEADS, HEAD_PAD, s), BF),
        jax.ShapeDtypeStruct((b, MLA_HEADS, s, HEAD_PAD), BF),
        jax.ShapeDtypeStruct((b, MLA_HEADS, VT_ROWS, s), BF),
    )
    out_specs = (
        pl.BlockSpec((1, tm, 3 * NA_WIDTH), lambda i, j: (i, j, 0)),
        pl.BlockSpec((1, tm, 2 * d), lambda i, j: (i, j, 0)),
        pl.BlockSpec((1, MLA_HEADS, HEAD_PAD, tm), lambda i, j: (i, 0, 0, j)),
        pl.BlockSpec((1, MLA_HEADS, tm, HEAD_PAD), lambda i, j: (i, 0, j, 0)),
        pl.BlockSpec((1, MLA_HEADS, VT_ROWS, tm), lambda i, j: (i, 0, 0, j)),
    )
    return pl.pallas_call(
        _proj_kernel, grid=grid, in_specs=in_specs, out_specs=out_specs, out_shape=out_shape,
        compiler_params=_params(("parallel", "parallel")), name="proj",
    )(x, *weights, *tabs)


def _na_kernel(q_ref, kp_ref, kc_ref, kn_ref, vp_ref, vc_ref, vn_ref, bias_ref, o_ref,
               kcat_ref, vcat_ref, *, rows, rb):
    i = pl.program_id(1)
    blk = rb * GRID_W
    win = NA_ROWS * GRID_W
    kcat_ref[0:blk] = kp_ref[0]
    kcat_ref[blk:2 * blk] = kc_ref[0]
    kcat_ref[2 * blk:3 * blk] = kn_ref[0]
    vcat_ref[0:blk] = vp_ref[0]
    vcat_ref[blk:2 * blk] = vc_ref[0]
    vcat_ref[2 * blk:3 * blk] = vn_ref[0]
    lane = lax.broadcasted_iota(jnp.int32, (GRID_W, LANES), 1)
    first_head = lane < NA_HEAD_DIM

    def row(a, carry):
        r = i * rb + a
        rs = jnp.clip(r - NA_ROWS // 2, 0, rows - NA_ROWS)
        off = pl.multiple_of((rs - (i - 1) * rb) * GRID_W, GRID_W)
        d = rs - r + (NA_ROWS - 1)
        qoff = pl.multiple_of(a * GRID_W, GRID_W)
        qrow = q_ref[0, pl.ds(qoff, GRID_W), :]
        zero = jnp.zeros((GRID_W, LANES), BF)
        sc = []
        for hp in range(NA_HEADS // 2):
            sl = slice(hp * LANES, (hp + 1) * LANES)
            qp = qrow[:, sl]
            qm = jnp.concatenate([jnp.where(first_head, qp, zero), jnp.where(first_head, zero, qp)], axis=0)
            sc.append(_dot_nt(qm, kcat_ref[pl.ds(off, win), sl]))
        sc = jnp.concatenate(sc, axis=0) + bias_ref[d]
        m = jnp.max(sc, axis=-1, keepdims=True)
        p = jnp.exp(sc - m)
        inv_l = 1.0 / jnp.sum(p, axis=-1, keepdims=True)
        pb = p.astype(BF)
        for hp in range(NA_HEADS // 2):
            sl = slice(hp * LANES, (hp + 1) * LANES)
            rows2 = slice(hp * 2 * GRID_W, (hp + 1) * 2 * GRID_W)
            o2 = _dot(pb[rows2], vcat_ref[pl.ds(off, win), sl]) * inv_l[rows2]
            o_ref[0, pl.ds(qoff, GRID_W), sl] = jnp.where(first_head, o2[:GRID_W], o2[GRID_W:]).astype(BF)
        return carry

    lax.fori_loop(0, rb, row, 0, unroll=2)


def _na_call(qkv, bias_tab, rb):
    b, s, _ = qkv.shape
    rows = s // GRID_W
    nb = rows // rb
    blk = rb * GRID_W

    def kv_spec(col, shift):
        return pl.BlockSpec((1, blk, NA_WIDTH),
                            lambda i, j: (i, jnp.clip(j + shift, 0, nb - 1), col))

    in_specs = [pl.BlockSpec((1, blk, NA_WIDTH), lambda i, j: (i, j, 0)),
                kv_spec(1, -1), kv_spec(1, 0), kv_spec(1, 1),
                kv_spec(2, -1), kv_spec(2, 0), kv_spec(2, 1),
                _const_spec(bias_tab.shape)]
    return pl.pallas_call(
        functools.partial(_na_kernel, rows=rows, rb=rb),
        grid=(b, nb), in_specs=in_specs,
        out_specs=pl.BlockSpec((1, blk, NA_WIDTH), lambda i, j: (i, j, 0)),
        out_shape=jax.ShapeDtypeStruct((b, s, NA_WIDTH), BF),
        scratch_shapes=[pltpu.VMEM((3 * blk, NA_WIDTH), BF), pltpu.VMEM((3 * blk, NA_WIDTH), BF)],
        compiler_params=_params(("parallel", "parallel")), name="na_attn",
    )(qkv, qkv, qkv, qkv, qkv, qkv, qkv, bias_tab)


def _na_bias_table(rpb):
    qc = np.arange(GRID_W)[:, None]
    kc = np.arange(GRID_W)[None, :]
    win_start = np.clip(qc - NA_COLS // 2, 0, GRID_W - NA_COLS)
    valid = (kc >= win_start) & (kc < win_start + NA_COLS)
    dc = np.clip(kc - qc + NA_COLS - 1, 0, 2 * NA_COLS - 2)
    tab = jnp.where(jnp.asarray(valid)[None, None], rpb[:, :, dc], MASK_VALUE)
    t = jnp.stack([tab[:, d:d + NA_ROWS] for d in range(NA_ROWS)], axis=0)
    t = t.transpose(0, 1, 3, 2, 4)
    return t.reshape(NA_ROWS, NA_HEADS * GRID_W, NA_ROWS * GRID_W).astype(F32)


def _mla_kernel(q_ref, k_ref, vt_ref, o_ref, s0_ref, s1_ref, mx_ref, acc_ref, m_ref, *, tk, unroll):
    qt = q_ref[0, 0]
    tq = qt.shape[1]
    nk = k_ref.shape[2] // tk
    s_refs = (s0_ref, s1_ref)

    def key_slice(c):
        return pl.ds(pl.multiple_of(c * tk, tk), tk)

    def scores(c, slot):
        st = _dot(k_ref[0, 0, key_slice(c), :], qt)
        s_refs[slot][...] = st
        mx_ref[slot] = jnp.max(st, axis=0, keepdims=True)

    def consume(c, slot):
        s_ref = s_refs[slot]
        vt = vt_ref[0, 0, :, key_slice(c)]
        for j in range(tq // MXU_COLS):
            cs = slice(j * MXU_COLS, (j + 1) * MXU_COLS)
            m = m_ref[:, cs]
            m_new = jnp.maximum(m, mx_ref[slot, :, cs])
            p = jnp.exp2(s_ref[:, cs] - m_new).astype(BF)
            acc_ref[:, cs] = acc_ref[:, cs] * jnp.exp2(m - m_new) + _dot(vt, p)
            m_ref[:, cs] = m_new

    scores(0, 0)
    m_ref[...] = jnp.full(m_ref.shape, MASK_VALUE, F32)
    acc_ref[...] = jnp.zeros(acc_ref.shape, F32)

    def body(i, carry):
        c0 = unroll * i
        for u in range(unroll):
            scores(c0 + u + 1, (u + 1) % 2)
            consume(c0 + u, u % 2)
        return carry

    lax.fori_loop(0, nk // unroll - 1, body, 0)
    c0 = nk - unroll
    for u in range(unroll):
        if u + 1 < unroll:
            scores(c0 + u + 1, (u + 1) % 2)
        consume(c0 + u, u % 2)
    acc = acc_ref[...]
    o_ref[0, 0] = (acc[0:MLA_V] / acc[MLA_V:MLA_V + 1]).astype(BF)


def _mla_call(q, k, vt, tq, tk, unroll):
    b, h, s, _ = k.shape
    assert unroll % 2 == 0 and (s // tk) % unroll == 0
    return pl.pallas_call(
        functools.partial(_mla_kernel, tk=tk, unroll=unroll),
        grid=(b, h, s // tq),
        in_specs=[pl.BlockSpec((1, 1, HEAD_PAD, tq), lambda i, j, t: (i, j, 0, t)),
                  pl.BlockSpec((1, 1, s, HEAD_PAD), lambda i, j, t: (i, j, 0, 0)),
                  pl.BlockSpec((1, 1, VT_ROWS, s), lambda i, j, t: (i, j, 0, 0))],
        out_specs=pl.BlockSpec((1, 1, MLA_V, tq), lambda i, j, t: (i, j, 0, t)),
        out_shape=jax.ShapeDtypeStruct((b, h, MLA_V, s), BF),
        scratch_shapes=[pltpu.VMEM((tk, tq), F32), pltpu.VMEM((tk, tq), F32), pltpu.VMEM((2, 1, tq), F32),
                        pltpu.VMEM((VT_ROWS, tq), F32), pltpu.VMEM((1, tq), F32)],
        compiler_params=_params(("parallel", "parallel", "parallel")), name="mla_attn",
    )(q, k, vt)


def _top2_route(x, wh, wl, bias):
    xh = x.astype(BF)
    xl = (x - xh.astype(F32)).astype(BF)
    lg = _dot_nt(wh, xh) + _dot_nt(wh, xl) + _dot_nt(wl, xh) + bias
    rid = lax.broadcasted_iota(jnp.int32, lg.shape, 0)
    pad = lg.shape[0]
    m1 = jnp.max(lg, axis=0, keepdims=True)
    i1 = jnp.min(jnp.where(lg == m1, rid, pad), axis=0, keepdims=True)
    lg2 = jnp.where(rid == i1, MASK_VALUE, lg)
    m2 = jnp.max(lg2, axis=0, keepdims=True)
    i2 = jnp.min(jnp.where(lg2 == m2, rid, pad), axis=0, keepdims=True)
    e = jnp.exp(m2 - m1)
    return jnp.concatenate([i1.astype(F32), i2.astype(F32), 1.0 / (1.0 + e), e / (1.0 + e),
                            jnp.zeros((4, lg.shape[1]), F32)], axis=0)


def _merge_kernel(x_ref, ona_ref, omt_ref, gate_ref, wna_ref, wmla_ref, wout_ref, g_ref, b_ref, *rest,
                  alpha, route):
    d = x_ref.shape[2]
    a = _dot(ona_ref[0], wna_ref[...])
    m = _dot_tn(omt_ref[0], wmla_ref[...])
    gate = gate_ref[0]
    merged = gate[:, :d].astype(F32) * a + gate[:, d:].astype(F32) * m
    y = alpha * x_ref[0] + _dot(merged.astype(BF), wout_ref[...])
    out = _layer_norm(y, g_ref[...], b_ref[...])
    if route:
        wh_ref, wl_ref, br_ref, o_ref, ob_ref, r_ref = rest
        ob_ref[0] = out.astype(BF)
        r_ref[...] = _top2_route(out, wh_ref[...], wl_ref[...], br_ref[...])
    else:
        (o_ref,) = rest
    o_ref[0] = out


def _merge_call(x, o_na, o_mla_t, gates, w, alpha, tm, route):
    b, s, d = x.shape
    nt = s // tm
    weights = [w['wnao'], w['wmlao'], w['wout'], w['ln1g'], w['ln1b']]
    tok = pl.BlockSpec((1, tm, d), lambda i, j: (i, j, 0))
    out_specs, out_shape = [tok], [jax.ShapeDtypeStruct((b, s, d), F32)]
    if route:
        weights += [w['wrh'], w['wrl'], w['br']]
        out_specs += [tok, pl.BlockSpec((8, tm), lambda i, j: (0, i * nt + j))]
        out_shape += [jax.ShapeDtypeStruct((b, s, d), BF), jax.ShapeDtypeStruct((8, b * s), F32)]
    in_specs = [tok,
                pl.BlockSpec((1, tm, NA_WIDTH), lambda i, j: (i, j, 0)),
                pl.BlockSpec((1, MLA_HEADS * MLA_V, tm), lambda i, j: (i, 0, j)),
                pl.BlockSpec((1, tm, 2 * d), lambda i, j: (i, j, 0))] + [_const_spec(a.shape) for a in weights]
    return pl.pallas_call(
        functools.partial(_merge_kernel, alpha=alpha, route=route),
        grid=(b, nt), in_specs=in_specs, out_specs=tuple(out_specs), out_shape=tuple(out_shape),
        compiler_params=_params(("parallel", "parallel")), name="merge_ln1",
    )(x, o_na, o_mla_t, gates, *weights)


def _ple_ln2(x, xb, f, p_ref, wpg_ref, wp_ref, g_ref, b_ref, alpha):
    ple = jax.nn.sigmoid(_dot(xb, wpg_ref[...])) * _dot(p_ref[0].astype(BF), wp_ref[...])
    return _layer_norm(alpha * x + f + ple, g_ref[...], b_ref[...])


def _ffn_kernel(x_ref, p_ref, w1_ref, w3_ref, w2_ref, wpg_ref, wp_ref, g_ref, b_ref, o_ref,
                *, alpha, chunks):
    x = x_ref[0]
    xb = x.astype(BF)
    fc = w1_ref.shape[1] // chunks
    f = None
    for c in range(chunks):
        sl = slice(c * fc, (c + 1) * fc)
        h = _silu(_dot(xb, w1_ref[:, sl])) * _dot(xb, w3_ref[:, sl])
        part = _dot(h.astype(BF), w2_ref[sl, :])
        f = part if f is None else f + part
    o_ref[0] = _ple_ln2(x, xb, f, p_ref, wpg_ref, wp_ref, g_ref, b_ref, alpha)


def _ffn_call(x, p, w, alpha, tm, chunks):
    b, s, d = x.shape
    weights = [w['w1'], w['w3'], w['w2'], w['wpg'], w['wp'], w['ln2g'], w['ln2b']]
    in_specs = [pl.BlockSpec((1, tm, d), lambda i, j: (i, j, 0)),
                pl.BlockSpec((1, tm, p.shape[-1]), lambda i, j: (i, j, 0))] + [_const_spec(a.shape) for a in weights]
    return pl.pallas_call(
        functools.partial(_ffn_kernel, alpha=alpha, chunks=chunks),
        grid=(b, s // tm), in_specs=in_specs,
        out_specs=pl.BlockSpec((1, tm, d), lambda i, j: (i, j, 0)),
        out_shape=jax.ShapeDtypeStruct((b, s, d), F32),
        compiler_params=_params(("parallel", "parallel")), name="ffn_ple_ln2",
    )(x, p, *weights)


def _experts_kernel(te_ref, na_ref, xs_ref, w1_ref, w3_ref, w2_ref, o_ref, *, chunks):
    active = pl.program_id(0) < na_ref[0]

    @pl.when(active)
    def _():
        xb = xs_ref[...]
        fc = w1_ref.shape[2] // chunks
        acc = None
        for c in range(chunks):
            sl = slice(c * fc, (c + 1) * fc)
            h = _silu(_dot(xb, w1_ref[0, :, sl])) * _dot(xb, w3_ref[0, :, sl])
            part = _dot(h.astype(BF), w2_ref[0, sl, :])
            acc = part if acc is None else acc + part
        o_ref[...] = acc.astype(BF)

    @pl.when(jnp.logical_not(active))
    def _():
        o_ref[...] = jnp.zeros_like(o_ref)


def _experts_call(tile_expert, n_active, xs, w1, w3, w2, tm, chunks):
    p_rows, d = xs.shape
    dff = w1.shape[2]

    def expert_spec(shape):
        return pl.BlockSpec(shape, lambda t, te, na: (te[t], 0, 0), pipeline_mode=pl.Buffered(1))

    grid_spec = pltpu.PrefetchScalarGridSpec(
        num_scalar_prefetch=2, grid=(p_rows // tm,),
        in_specs=[pl.BlockSpec((tm, d), lambda t, te, na: (t, 0)),
                  expert_spec((1, d, dff)), expert_spec((1, d, dff)), expert_spec((1, dff, d))],
        out_specs=pl.BlockSpec((tm, d), lambda t, te, na: (t, 0)))
    return pl.pallas_call(
        functools.partial(_experts_kernel, chunks=chunks), grid_spec=grid_spec,
        out_shape=jax.ShapeDtypeStruct((p_rows, d), BF),
        compiler_params=_params(("arbitrary",)), name="experts",
    )(tile_expert, n_active, xs, w1, w3, w2)


def _combine_kernel(x_ref, ya_ref, yb_ref, gw_ref, p_ref, wpg_ref, wp_ref, g_ref, b_ref, o_ref, *, alpha):
    x = x_ref[0]
    gw = gw_ref[0]
    f = gw[:, 0:1] * ya_ref[0].astype(F32) + gw[:, 1:2] * yb_ref[0].astype(F32)
    o_ref[0] = _ple_ln2(x, x.astype(BF), f, p_ref, wpg_ref, wp_ref, g_ref, b_ref, alpha)


def _combine_call(x, ya, yb, gw, p, w, alpha, tm):
    b, s, d = x.shape
    weights = [w['wpg'], w['wp'], w['ln2g'], w['ln2b']]
    tok = lambda width: pl.BlockSpec((1, tm, width), lambda i, j: (i, j, 0))
    in_specs = [tok(d), tok(d), tok(d), tok(2), tok(p.shape[-1])] + [_const_spec(a.shape) for a in weights]
    return pl.pallas_call(
        functools.partial(_combine_kernel, alpha=alpha),
        grid=(b, s // tm), in_specs=in_specs, out_specs=tok(d),
        out_shape=jax.ShapeDtypeStruct((b, s, d), F32),
        compiler_params=_params(("parallel", "parallel")), name="moe_combine_ple_ln2",
    )(x, ya, yb, gw, p, *weights)


def _moe_layer(x1, x1b, route, p, w, alpha, tl):
    b, s, d = x1.shape
    n = b * s
    tm = tl['moe_tm']
    e_all = route[0:2].astype(jnp.int32).reshape(2 * n)
    gw = route[2:4].T.reshape(b, s, 2)

    onehot = (e_all[:, None] == jnp.arange(N_EXPERTS, dtype=jnp.int32)[None, :]).astype(jnp.int32)
    csum = jnp.cumsum(onehot, axis=0)
    rank = jnp.sum(onehot * (csum - 1), axis=1)
    counts = csum[-1]
    padded = ((counts + tm - 1) // tm) * tm
    ends = jnp.cumsum(padded)
    starts = ends - padded
    dest = starts[e_all] + rank
    p_rows = 2 * n + N_EXPERTS * tm
    n_active = (ends[-1] // tm).astype(jnp.int32).reshape(1)
    tile_start = jnp.minimum(jnp.arange(p_rows // tm, dtype=jnp.int32), n_active[0] - 1) * tm
    tile_expert = jnp.sum((tile_start[:, None] >= ends[None, :]).astype(jnp.int32), axis=1)
    tile_expert = jnp.minimum(tile_expert, N_EXPERTS - 1).astype(jnp.int32)
    order = jnp.argsort(e_all, stable=True).astype(jnp.int32)
    row_expert = jnp.repeat(tile_expert, tm)
    row_rank = jnp.arange(p_rows, dtype=jnp.int32) - starts[row_expert]
    src = jnp.clip((jnp.cumsum(counts) - counts)[row_expert] + row_rank, 0, 2 * n - 1)
    row_token = jnp.where(row_rank < counts[row_expert], order[src] % n, 0)

    xs = x1b.reshape(n, d).at[row_token].get(mode='promise_in_bounds')
    ys = _experts_call(tile_expert, n_active, xs, w['mw1'], w['mw3'], w['mw2'], tm, tl['moe_chunks'])
    ya = ys.at[dest[:n]].get(mode='promise_in_bounds').reshape(b, s, d)
    yb = ys.at[dest[n:]].get(mode='promise_in_bounds').reshape(b, s, d)
    return _combine_call(x1, ya, yb, gw, p, w, alpha, tl['tm'])


def _rope_tables(s):
    pos = jnp.arange(s, dtype=F32)
    inv_freq = ROPE_THETA ** (-jnp.arange(0, MLA_ROPE // 2, dtype=F32) * (2.0 / MLA_ROPE))
    ang = pos[:, None] * inv_freq[None, :]
    cos, sin = jnp.cos(ang), jnp.sin(ang)
    ck = jnp.concatenate([jnp.ones((s, MLA_NOPE), F32), cos, cos,
                          jnp.zeros((s, HEAD_PAD - MLA_NOPE - MLA_ROPE), F32)], axis=1)
    sk = jnp.concatenate([jnp.zeros((s, MLA_NOPE), F32), sin, sin,
                          jnp.zeros((s, HEAD_PAD - MLA_NOPE - MLA_ROPE), F32)], axis=1)
    qs = (MLA_NOPE + MLA_ROPE) ** -0.5 * math.log2(math.e)
    return (ck * qs).T, (sk * qs).T, ck, sk


def _rotate_half_cols(w_rope):
    half = w_rope.shape[-1] // 2
    return jnp.concatenate([-w_rope[..., half:], w_rope[..., :half]], axis=-1)


def _prep_layer(i, w_in, b_gate, q_norm_g, w_uq, kv_norm_g, w_ukv, w_na_o, w_mla_o, w_out, ln1_g, ln1_b,
                w_ple_gate, w_ple, ln2_g, ln2_b):
    d = w_in.shape[1]
    o = 0
    wna = w_in[i, :, o:o + 3 * NA_WIDTH]; o += 3 * NA_WIDTH
    wql = w_in[i, :, o:o + MLA_Q_LORA]; o += MLA_Q_LORA
    wkvl = w_in[i, :, o:o + MLA_KV_LORA]; o += MLA_KV_LORA
    wkr = w_in[i, :, o:o + MLA_ROPE]; o += MLA_ROPE
    wg = w_in[i, :, o:o + 2 * d]
    na_scale = jnp.concatenate([jnp.full((NA_WIDTH,), NA_HEAD_DIM ** -0.5, F32), jnp.ones((2 * NA_WIDTH,), F32)])
    wna = wna * na_scale[None, :]

    zpad = HEAD_PAD - MLA_NOPE - MLA_ROPE
    uq = w_uq[i].reshape(MLA_Q_LORA, MLA_HEADS, MLA_NOPE + MLA_ROPE)
    assert zpad == MLA_ROPE
    uq_pad = jnp.concatenate([uq, _rotate_half_cols(uq[..., MLA_NOPE:])], axis=-1)
    kr_pad = jnp.pad(wkr, ((0, 0), (MLA_NOPE, zpad)))
    kr_rot = jnp.pad(_rotate_half_cols(wkr), ((0, 0), (MLA_NOPE, zpad)))
    ukv = w_ukv[i].reshape(MLA_KV_LORA, MLA_HEADS, MLA_NOPE + MLA_V)
    uk_pad = jnp.pad(ukv[..., :MLA_NOPE], ((0, 0), (0, 0), (0, HEAD_PAD - MLA_NOPE)))
    uvt = ukv[..., MLA_NOPE:].reshape(MLA_KV_LORA, MLA_HEADS * MLA_V).T

    return dict(
        wna=wna.astype(BF), wg=wg.astype(BF), bg=b_gate[i][None, :], wql=wql.astype(BF), wkvl=wkvl.astype(BF),
        wkr=jnp.concatenate([kr_pad, kr_rot], axis=1).astype(BF),
        gq=q_norm_g[i][None, :], gkv=kv_norm_g[i][None, :],
        wuq=uq_pad.reshape(MLA_Q_LORA, MLA_HEADS * HEAD_PAD).T.astype(BF),
        wuk=uk_pad.reshape(MLA_KV_LORA, MLA_HEADS * HEAD_PAD).astype(BF), wuvt=uvt.astype(BF),
        wnao=w_na_o[i].astype(BF), wmlao=w_mla_o[i].astype(BF), wout=w_out[i].astype(BF),
        ln1g=ln1_g[i][None, :], ln1b=ln1_b[i][None, :],
        wpg=w_ple_gate[i].astype(BF), wp=w_ple[i].astype(BF), ln2g=ln2_g[i][None, :], ln2b=ln2_b[i][None, :],
    )


def kernel(x, p, w_in, b_gate, q_norm_g, w_uq, kv_norm_g, w_ukv, na_rpb, w_na_o, w_mla_o, w_out, ln1_g, ln1_b,
           ffn_w1, ffn_w3, ffn_w2, moe_w_router, moe_b_router, moe_w1, moe_w3, moe_w2, w_ple_gate, w_ple,
           ln2_g, ln2_b):
    b, s, d = x.shape
    depth = w_in.shape[0]
    alpha = (2 * depth) ** 0.25
    tl = _tiles(b * s, s)
    tabs = _rope_tables(s)

    for i in range(depth):
        w = _prep_layer(i, w_in, b_gate, q_norm_g, w_uq, kv_norm_g, w_ukv, w_na_o, w_mla_o, w_out, ln1_g, ln1_b,
                        w_ple_gate, w_ple, ln2_g, ln2_b)
        qkv, gates, q_mla, k_mla, vt = _proj_call(x, w, tabs, tl['tm'])
        o_na = _na_call(qkv, _na_bias_table(na_rpb[i]), tl['na_rows'])
        o_mla_t = _mla_call(q_mla, k_mla, vt, tl['tq'], tl['tk'], tl['mla_unroll'])
        o_mla_t = o_mla_t.reshape(b, MLA_HEADS * MLA_V, s)
        j = i // 2
        if i % 2 == 0:
            (x1,) = _merge_call(x, o_na, o_mla_t, gates, w, alpha, tl['tm'], route=False)
            w.update(w1=ffn_w1[j].astype(BF), w3=ffn_w3[j].astype(BF), w2=ffn_w2[j].astype(BF))
            x = _ffn_call(x1, p[i], w, alpha, tl['tm'], tl['ffn_chunks'])
        else:
            wr = jnp.pad(moe_w_router[j].T, ((0, BF16_SUBLANES - N_EXPERTS), (0, 0)))
            wrh = wr.astype(BF)
            br = jnp.concatenate([moe_b_router[j], jnp.full((BF16_SUBLANES - N_EXPERTS,), MASK_VALUE, F32)])
            w.update(wrh=wrh, wrl=(wr - wrh.astype(F32)).astype(BF), br=br[:, None],
                     mw1=moe_w1[j].astype(BF), mw3=moe_w3[j].astype(BF), mw2=moe_w2[j].astype(BF))
            x1, x1b, route = _merge_call(x, o_na, o_mla_t, gates, w, alpha, tl['tm'], route=True)
            x = _moe_layer(x1, x1b, route, p[i], w, alpha, tl)
    return x
```

```python
import functools
import math

import numpy as np
import jax
import jax.numpy as jnp
from jax import lax
from jax.experimental import pallas as pl
from jax.experimental.pallas import tpu as pltpu

BF = jnp.bfloat16
F32 = jnp.float32

GRID_W = 64
NA_HEADS = 8
NA_HEAD_DIM = 64
NA_ROWS = 8
NA_COLS = 16
NA_WIDTH = NA_HEADS * NA_HEAD_DIM
MLA_HEADS = 8
MLA_NOPE = 64
MLA_ROPE = 32
MLA_V = 64
MLA_Q_LORA = 768
MLA_KV_LORA = 256
ROPE_THETA = 10000.0
N_EXPERTS = 8
LN_EPS = 1e-5
RMS_EPS = 1e-6
MASK_VALUE = -1e30

LANES = 128
BF16_SUBLANES = 16
MXU_COLS = 256
VMEM_LIMIT_BYTES = 56 * 1024 * 1024

HEAD_PAD = LANES
VT_ROWS = MLA_V + BF16_SUBLANES


def _dot(a, b):
    return jnp.dot(a, b, preferred_element_type=F32)


def _dot_nt(a, b):
    return lax.dot_general(a, b, (((1,), (1,)), ((), ())), preferred_element_type=F32)


def _dot_tn(a, b):
    return lax.dot_general(a, b, (((0,), (0,)), ((), ())), preferred_element_type=F32)


def _const_spec(shape):
    nd = len(shape)
    return pl.BlockSpec(shape, lambda *_: (0,) * nd, pipeline_mode=pl.Buffered(1))


def _params(sem):
    return pltpu.CompilerParams(dimension_semantics=sem, vmem_limit_bytes=VMEM_LIMIT_BYTES)


def _tiles(n_tokens, seq):
    return dict(
        tm=min(512, seq),
        tq=min(1024, seq),
        tk=min(512, seq),
        mla_unroll=4 if (seq // min(512, seq)) % 4 == 0 else 2,
        na_rows=8,
        moe_tm=512 if n_tokens >= 8192 else 128,
        moe_chunks=2,
        ffn_chunks=1,
    )


def _layer_norm(y, g, b):
    mu = jnp.mean(y, axis=-1, keepdims=True)
    yc = y - mu
    var = jnp.mean(yc * yc, axis=-1, keepdims=True)
    return yc * lax.rsqrt(var + LN_EPS) * g + b


def _rms_norm(y, g):
    return y * lax.rsqrt(jnp.mean(y * y, axis=-1, keepdims=True) + RMS_EPS) * g


def _silu(y):
    return y * jax.nn.sigmoid(y)


def _proj_kernel(x_ref, wna_ref, wg_ref, bg_ref, wql_ref, wkvl_ref, wkr_ref, gq_ref, gkv_ref,
                 wuq_ref, wuk_ref, wuvt_ref, cq_ref, sq_ref, ck_ref, sk_ref,
                 qkv_ref, gate_ref, q_ref, k_ref, vt_ref):
    xb = x_ref[0].astype(BF)
    qkv_ref[0] = _dot(xb, wna_ref[...]).astype(BF)
    gate_ref[0] = jax.nn.sigmoid(_dot(xb, wg_ref[...]) + bg_ref[...]).astype(BF)

    qn = _rms_norm(_dot(xb, wql_ref[...]), gq_ref[...]).astype(BF)
    qat = _dot_nt(wuq_ref[...], qn)
    cq, sq = cq_ref[...], sq_ref[...]
    for h in range(MLA_HEADS):
        qh = qat[h * HEAD_PAD:(h + 1) * HEAD_PAD]
        partner = pltpu.roll(qh, shift=HEAD_PAD - MLA_ROPE, axis=0)
        q_ref[0, h] = (qh * cq + partner * sq).astype(BF)

    kvn = _rms_norm(_dot(xb, wkvl_ref[...]), gkv_ref[...]).astype(BF)
    kr2 = _dot(xb, wkr_ref[...])
    kr = kr2[:, :HEAD_PAD] * ck_ref[...] + kr2[:, HEAD_PAD:] * sk_ref[...]
    kn = _dot(kvn, wuk_ref[...])
    for h in range(MLA_HEADS):
        sl = slice(h * HEAD_PAD, (h + 1) * HEAD_PAD)
        k_ref[0, h] = (kn[:, sl] + kr).astype(BF)

    vt = _dot_nt(wuvt_ref[...], kvn)
    ones = jnp.ones((BF16_SUBLANES, vt.shape[1]), BF)
    for h in range(MLA_HEADS):
        vt_ref[0, h, 0:MLA_V, :] = vt[h * MLA_V:(h + 1) * MLA_V].astype(BF)
        vt_ref[0, h, MLA_V:VT_ROWS, :] = ones


def _proj_call(x, w, tabs, tm):
    b, s, d = x.shape
    grid = (b, s // tm)
    weights = [w['wna'], w['wg'], w['bg'], w['wql'], w['wkvl'], w['wkr'], w['gq'], w['gkv'],
               w['wuq'], w['wuk'], w['wuvt']]
    tab_spec = pl.BlockSpec((tm, HEAD_PAD), lambda i, j: (j, 0))
    tab_t_spec = pl.BlockSpec((HEAD_PAD, tm), lambda i, j: (0, j))
    in_specs = ([pl.BlockSpec((1, tm, d), lambda i, j: (i, j, 0))]
                + [_const_spec(a.shape) for a in weights] + [tab_t_spec] * 2 + [tab_spec] * 2)
    out_shape = (
        jax.ShapeDtypeStruct((b, s, 3 * NA_WIDTH), BF),
        jax.ShapeDtypeStruct((b, s, 2 * d), BF),
        jax.ShapeDtypeStruct((b, MLA_HEADS, HEAD_PAD, s), BF),
        jax.ShapeDtypeStruct((b, MLA_HEADS, s, HEAD_PAD), BF),
        jax.ShapeDtypeStruct((b, MLA_HEADS, VT_ROWS, s), BF),
    )
    out_specs = (
        pl.BlockSpec((1, tm, 3 * NA_WIDTH), lambda i, j: (i, j, 0)),
        pl.BlockSpec((1, tm, 2 * d), lambda i, j: (i, j, 0)),
        pl.BlockSpec((1, MLA_HEADS, HEAD_PAD, tm), lambda i, j: (i, 0, 0, j)),
        pl.BlockSpec((1, MLA_HEADS, tm, HEAD_PAD), lambda i, j: (i, 0, j, 0)),
        pl.BlockSpec((1, MLA_HEADS, VT_ROWS, tm), lambda i, j: (i, 0, 0, j)),
    )
    return pl.pallas_call(
        _proj_kernel, grid=grid, in_specs=in_specs, out_specs=out_specs, out_shape=out_shape,
        compiler_params=_params(("parallel", "parallel")), name="proj",
    )(x, *weights, *tabs)


def _na_kernel(q_ref, kp_ref, kc_ref, kn_ref, vp_ref, vc_ref, vn_ref, bias_ref, o_ref,
               kcat_ref, vcat_ref, *, rows, rb):
    i = pl.program_id(1)
    blk = rb * GRID_W
    win = NA_ROWS * GRID_W
    kcat_ref[0:blk] = kp_ref[0]
    kcat_ref[blk:2 * blk] = kc_ref[0]
    kcat_ref[2 * blk:3 * blk] = kn_ref[0]
    vcat_ref[0:blk] = vp_ref[0]
    vcat_ref[blk:2 * blk] = vc_ref[0]
    vcat_ref[2 * blk:3 * blk] = vn_ref[0]
    lane = lax.broadcasted_iota(jnp.int32, (GRID_W, LANES), 1)
    first_head = lane < NA_HEAD_DIM

    def row(a, carry):
        r = i * rb + a
        rs = jnp.clip(r - NA_ROWS // 2, 0, rows - NA_ROWS)
        off = pl.multiple_of((rs - (i - 1) * rb) * GRID_W, GRID_W)
        d = rs - r + (NA_ROWS - 1)
        qoff = pl.multiple_of(a * GRID_W, GRID_W)
        qrow = q_ref[0, pl.ds(qoff, GRID_W), :]
        zero = jnp.zeros((GRID_W, LANES), BF)
        sc = []
        for hp in range(NA_HEADS // 2):
            sl = slice(hp * LANES, (hp + 1) * LANES)
            qp = qrow[:, sl]
            qm = jnp.concatenate([jnp.where(first_head, qp, zero), jnp.where(first_head, zero, qp)], axis=0)
            sc.append(_dot_nt(qm, kcat_ref[pl.ds(off, win), sl]))
        sc = jnp.concatenate(sc, axis=0) + bias_ref[d]
        m = jnp.max(sc, axis=-1, keepdims=True)
        p = jnp.exp(sc - m)
        inv_l = 1.0 / jnp.sum(p, axis=-1, keepdims=True)
        pb = p.astype(BF)
        for hp in range(NA_HEADS // 2):
            sl = slice(hp * LANES, (hp + 1) * LANES)
            rows2 = slice(hp * 2 * GRID_W, (hp + 1) * 2 * GRID_W)
            o2 = _dot(pb[rows2], vcat_ref[pl.ds(off, win), sl]) * inv_l[rows2]
            o_ref[0, pl.ds(qoff, GRID_W), sl] = jnp.where(first_head, o2[:GRID_W], o2[GRID_W:]).astype(BF)
        return carry

    lax.fori_loop(0, rb, row, 0, unroll=2)


def _na_call(qkv, bias_tab, rb):
    b, s, _ = qkv.shape
    rows = s // GRID_W
    nb = rows // rb
    blk = rb * GRID_W

    def kv_spec(col, shift):
        return pl.BlockSpec((1, blk, NA_WIDTH),
                            lambda i, j: (i, jnp.clip(j + shift, 0, nb - 1), col))

    in_specs = [pl.BlockSpec((1, blk, NA_WIDTH), lambda i, j: (i, j, 0)),
                kv_spec(1, -1), kv_spec(1, 0), kv_spec(1, 1),
                kv_spec(2, -1), kv_spec(2, 0), kv_spec(2, 1),
                _const_spec(bias_tab.shape)]
    return pl.pallas_call(
        functools.partial(_na_kernel, rows=rows, rb=rb),
        grid=(b, nb), in_specs=in_specs,
        out_specs=pl.BlockSpec((1, blk, NA_WIDTH), lambda i, j: (i, j, 0)),
        out_shape=jax.ShapeDtypeStruct((b, s, NA_WIDTH), BF),
        scratch_shapes=[pltpu.VMEM((3 * blk, NA_WIDTH), BF), pltpu.VMEM((3 * blk, NA_WIDTH), BF)],
        compiler_params=_params(("parallel", "parallel")), name="na_attn",
    )(qkv, qkv, qkv, qkv, qkv, qkv, qkv, bias_tab)


def _na_bias_table(rpb):
    qc = np.arange(GRID_W)[:, None]
    kc = np.arange(GRID_W)[None, :]
    win_start = np.clip(qc - NA_COLS // 2, 0, GRID_W - NA_COLS)
    valid = (kc >= win_start) & (kc < win_start + NA_COLS)
    dc = np.clip(kc - qc + NA_COLS - 1, 0, 2 * NA_COLS - 2)
    tab = jnp.where(jnp.asarray(valid)[None, None], rpb[:, :, dc], MASK_VALUE)
    t = jnp.stack([tab[:, d:d + NA_ROWS] for d in range(NA_ROWS)], axis=0)
    t = t.transpose(0, 1, 3, 2, 4)
    return t.reshape(NA_ROWS, NA_HEADS * GRID_W, NA_ROWS * GRID_W).astype(F32)


def _mla_kernel(q_ref, k_ref, vt_ref, o_ref, s0_ref, s1_ref, mx_ref, acc_ref, m_ref, *, tk, unroll):
    qt = q_ref[0, 0]
    tq = qt.shape[1]
    nk = k_ref.shape[2] // tk
    s_refs = (s0_ref, s1_ref)

    def key_slice(c):
        return pl.ds(pl.multiple_of(c * tk, tk), tk)

    def scores(c, slot):
        st = _dot(k_ref[0, 0, key_slice(c), :], qt)
        s_refs[slot][...] = st
        mx_ref[slot] = jnp.max(st, axis=0, keepdims=True)

    def consume(c, slot):
        s_ref = s_refs[slot]
        vt = vt_ref[0, 0, :, key_slice(c)]
        for j in range(tq // MXU_COLS):
            cs = slice(j * MXU_COLS, (j + 1) * MXU_COLS)
            m = m_ref[:, cs]
            m_new = jnp.maximum(m, mx_ref[slot, :, cs])
            p = jnp.exp2(s_ref[:, cs] - m_new).astype(BF)
            acc_ref[:, cs] = acc_ref[:, cs] * jnp.exp2(m - m_new) + _dot(vt, p)
            m_ref[:, cs] = m_new

    scores(0, 0)
    m_ref[...] = jnp.full(m_ref.shape, MASK_VALUE, F32)
    acc_ref[...] = jnp.zeros(acc_ref.shape, F32)

    def body(i, carry):
        c0 = unroll * i
        for u in range(unroll):
            scores(c0 + u + 1, (u + 1) % 2)
            consume(c0 + u, u % 2)
        return carry

    lax.fori_loop(0, nk // unroll - 1, body, 0)
    c0 = nk - unroll
    for u in range(unroll):
        if u + 1 < unroll:
            scores(c0 + u + 1, (u + 1) % 2)
        consume(c0 + u, u % 2)
    acc = acc_ref[...]
    o_ref[0, 0] = (acc[0:MLA_V] / acc[MLA_V:MLA_V + 1]).astype(BF)


def _mla_call(q, k, vt, tq, tk, unroll):
    b, h, s, _ = k.shape
    assert unroll % 2 == 0 and (s // tk) % unroll == 0
    return pl.pallas_call(
        functools.partial(_mla_kernel, tk=tk, unroll=unroll),
        grid=(b, h, s // tq),
        in_specs=[pl.BlockSpec((1, 1, HEAD_PAD, tq), lambda i, j, t: (i, j, 0, t)),
                  pl.BlockSpec((1, 1, s, HEAD_PAD), lambda i, j, t: (i, j, 0, 0)),
                  pl.BlockSpec((1, 1, VT_ROWS, s), lambda i, j, t: (i, j, 0, 0))],
        out_specs=pl.BlockSpec((1, 1, MLA_V, tq), lambda i, j, t: (i, j, 0, t)),
        out_shape=jax.ShapeDtypeStruct((b, h, MLA_V, s), BF),
        scratch_shapes=[pltpu.VMEM((tk, tq), F32), pltpu.VMEM((tk, tq), F32), pltpu.VMEM((2, 1, tq), F32),
                        pltpu.VMEM((VT_ROWS, tq), F32), pltpu.VMEM((1, tq), F32)],
        compiler_params=_params(("parallel", "parallel", "parallel")), name="mla_attn",
    )(q, k, vt)


def _top2_route(x, wh, wl, bias):
    xh = x.astype(BF)
    xl = (x - xh.astype(F32)).astype(BF)
    lg = _dot_nt(wh, xh) + _dot_nt(wh, xl) + _dot_nt(wl, xh) + bias
    rid = lax.broadcasted_iota(jnp.int32, lg.shape, 0)
    pad = lg.shape[0]
    m1 = jnp.max(lg, axis=0, keepdims=True)
    i1 = jnp.min(jnp.where(lg == m1, rid, pad), axis=0, keepdims=True)
    lg2 = jnp.where(rid == i1, MASK_VALUE, lg)
    m2 = jnp.max(lg2, axis=0, keepdims=True)
    i2 = jnp.min(jnp.where(lg2 == m2, rid, pad), axis=0, keepdims=True)
    e = jnp.exp(m2 - m1)
    return jnp.concatenate([i1.astype(F32), i2.astype(F32), 1.0 / (1.0 + e), e / (1.0 + e),
                            jnp.zeros((4, lg.shape[1]), F32)], axis=0)


def _merge_kernel(x_ref, ona_ref, omt_ref, gate_ref, wna_ref, wmla_ref, wout_ref, g_ref, b_ref, *rest,
                  alpha, route):
    d = x_ref.shape[2]
    a = _dot(ona_ref[0], wna_ref[...])
    m = _dot_tn(omt_ref[0], wmla_ref[...])
    gate = gate_ref[0]
    merged = gate[:, :d].astype(F32) * a + gate[:, d:].astype(F32) * m
    y = alpha * x_ref[0] + _dot(merged.astype(BF), wout_ref[...])
    out = _layer_norm(y, g_ref[...], b_ref[...])
    if route:
        wh_ref, wl_ref, br_ref, o_ref, ob_ref, r_ref = rest
        ob_ref[0] = out.astype(BF)
        r_ref[...] = _top2_route(out, wh_ref[...], wl_ref[...], br_ref[...])
    else:
        (o_ref,) = rest
    o_ref[0] = out


def _merge_call(x, o_na, o_mla_t, gates, w, alpha, tm, route):
    b, s, d = x.shape
    nt = s // tm
    weights = [w['wnao'], w['wmlao'], w['wout'], w['ln1g'], w['ln1b']]
    tok = pl.BlockSpec((1, tm, d), lambda i, j: (i, j, 0))
    out_specs, out_shape = [tok], [jax.ShapeDtypeStruct((b, s, d), F32)]
    if route:
        weights += [w['wrh'], w['wrl'], w['br']]
        out_specs += [tok, pl.BlockSpec((8, tm), lambda i, j: (0, i * nt + j))]
        out_shape += [jax.ShapeDtypeStruct((b, s, d), BF), jax.ShapeDtypeStruct((8, b * s), F32)]
    in_specs = [tok,
                pl.BlockSpec((1, tm, NA_WIDTH), lambda i, j: (i, j, 0)),
                pl.BlockSpec((1, MLA_HEADS * MLA_V, tm), lambda i, j: (i, 0, j)),
                pl.BlockSpec((1, tm, 2 * d), lambda i, j: (i, j, 0))] + [_const_spec(a.shape) for a in weights]
    return pl.pallas_call(
        functools.partial(_merge_kernel, alpha=alpha, route=route),
        grid=(b, nt), in_specs=in_specs, out_specs=tuple(out_specs), out_shape=tuple(out_shape),
        compiler_params=_params(("parallel", "parallel")), name="merge_ln1",
    )(x, o_na, o_mla_t, gates, *weights)


def _ple_ln2(x, xb, f, p_ref, wpg_ref, wp_ref, g_ref, b_ref, alpha):
    ple = jax.nn.sigmoid(_dot(xb, wpg_ref[...])) * _dot(p_ref[0].astype(BF), wp_ref[...])
    return _layer_norm(alpha * x + f + ple, g_ref[...], b_ref[...])


def _ffn_kernel(x_ref, p_ref, w1_ref, w3_ref, w2_ref, wpg_ref, wp_ref, g_ref, b_ref, o_ref,
                *, alpha, chunks):
    x = x_ref[0]
    xb = x.astype(BF)
    fc = w1_ref.shape[1] // chunks
    f = None
    for c in range(chunks):
        sl = slice(c * fc, (c + 1) * fc)
        h = _silu(_dot(xb, w1_ref[:, sl])) * _dot(xb, w3_ref[:, sl])
        part = _dot(h.astype(BF), w2_ref[sl, :])
        f = part if f is None else f + part
    o_ref[0] = _ple_ln2(x, xb, f, p_ref, wpg_ref, wp_ref, g_ref, b_ref, alpha)


def _ffn_call(x, p, w, alpha, tm, chunks):
    b, s, d = x.shape
    weights = [w['w1'], w['w3'], w['w2'], w['wpg'], w['wp'], w['ln2g'], w['ln2b']]
    in_specs = [pl.BlockSpec((1, tm, d), lambda i, j: (i, j, 0)),
                pl.BlockSpec((1, tm, p.shape[-1]), lambda i, j: (i, j, 0))] + [_const_spec(a.shape) for a in weights]
    return pl.pallas_call(
        functools.partial(_ffn_kernel, alpha=alpha, chunks=chunks),
        grid=(b, s // tm), in_specs=in_specs,
        out_specs=pl.BlockSpec((1, tm, d), lambda i, j: (i, j, 0)),
        out_shape=jax.ShapeDtypeStruct((b, s, d), F32),
        compiler_params=_params(("parallel", "parallel")), name="ffn_ple_ln2",
    )(x, p, *weights)


def _experts_kernel(te_ref, na_ref, xs_ref, w1_ref, w3_ref, w2_ref, o_ref, *, chunks):
    active = pl.program_id(0) < na_ref[0]

    @pl.when(active)
    def _():
        xb = xs_ref[...]
        fc = w1_ref.shape[2] // chunks
        acc = None
        for c in range(chunks):
            sl = slice(c * fc, (c + 1) * fc)
            h = _silu(_dot(xb, w1_ref[0, :, sl])) * _dot(xb, w3_ref[0, :, sl])
            part = _dot(h.astype(BF), w2_ref[0, sl, :])
            acc = part if acc is None else acc + part
        o_ref[...] = acc.astype(BF)

    @pl.when(jnp.logical_not(active))
    def _():
        o_ref[...] = jnp.zeros_like(o_ref)


def _experts_call(tile_expert, n_active, xs, w1, w3, w2, tm, chunks):
    p_rows, d = xs.shape
    dff = w1.shape[2]

    def expert_spec(shape):
        return pl.BlockSpec(shape, lambda t, te, na: (te[t], 0, 0), pipeline_mode=pl.Buffered(1))

    grid_spec = pltpu.PrefetchScalarGridSpec(
        num_scalar_prefetch=2, grid=(p_rows // tm,),
        in_specs=[pl.BlockSpec((tm, d), lambda t, te, na: (t, 0)),
                  expert_spec((1, d, dff)), expert_spec((1, d, dff)), expert_spec((1, dff, d))],
        out_specs=pl.BlockSpec((tm, d), lambda t, te, na: (t, 0)))
    return pl.pallas_call(
        functools.partial(_experts_kernel, chunks=chunks), grid_spec=grid_spec,
        out_shape=jax.ShapeDtypeStruct((p_rows, d), BF),
        compiler_params=_params(("arbitrary",)), name="experts",
    )(tile_expert, n_active, xs, w1, w3, w2)


def _combine_kernel(x_ref, ya_ref, yb_ref, gw_ref, p_ref, wpg_ref, wp_ref, g_ref, b_ref, o_ref, *, alpha):
    x = x_ref[0]
    gw = gw_ref[0]
    f = gw[:, 0:1] * ya_ref[0, 0].astype(F32) + gw[:, 1:2] * yb_ref[0, 0].astype(F32)
    o_ref[0] = _ple_ln2(x, x.astype(BF), f, p_ref, wpg_ref, wp_ref, g_ref, b_ref, alpha)


def _combine_call(x, y2, gw, p, w, alpha, tm):
    b, s, d = x.shape
    weights = [w['wpg'], w['wp'], w['ln2g'], w['ln2b']]
    tok = lambda width: pl.BlockSpec((1, tm, width), lambda i, j: (i, j, 0))
    choice = lambda k: pl.BlockSpec((1, 1, tm, d), lambda i, j: (k, i, j, 0))
    in_specs = ([tok(d), choice(0), choice(1), tok(2), tok(p.shape[-1])]
                + [_const_spec(a.shape) for a in weights])
    return pl.pallas_call(
        functools.partial(_combine_kernel, alpha=alpha),
        grid=(b, s // tm), in_specs=in_specs, out_specs=tok(d),
        out_shape=jax.ShapeDtypeStruct((b, s, d), F32),
        compiler_params=_params(("parallel", "parallel")), name="moe_combine_ple_ln2",
    )(x, y2, y2, gw, p, *weights)


def _moe_layer(x1, x1b, route, p, w, alpha, tl):
    b, s, d = x1.shape
    n = b * s
    tm = tl['moe_tm']
    e_all = route[0:2].astype(jnp.int32).reshape(2 * n)
    gw = route[2:4].T.reshape(b, s, 2)

    onehot = (e_all[:, None] == jnp.arange(N_EXPERTS, dtype=jnp.int32)[None, :]).astype(jnp.int32)
    csum = jnp.cumsum(onehot, axis=0)
    rank = jnp.sum(onehot * (csum - 1), axis=1)
    counts = csum[-1]
    padded = ((counts + tm - 1) // tm) * tm
    ends = jnp.cumsum(padded)
    starts = ends - padded
    dest = starts[e_all] + rank
    p_rows = 2 * n + N_EXPERTS * tm
    n_active = (ends[-1] // tm).astype(jnp.int32).reshape(1)
    tile_start = jnp.minimum(jnp.arange(p_rows // tm, dtype=jnp.int32), n_active[0] - 1) * tm
    tile_expert = jnp.sum((tile_start[:, None] >= ends[None, :]).astype(jnp.int32), axis=1)
    tile_expert = jnp.minimum(tile_expert, N_EXPERTS - 1).astype(jnp.int32)
    order = jnp.argsort(e_all, stable=True).astype(jnp.int32)
    row_expert = jnp.repeat(tile_expert, tm)
    row_rank = jnp.arange(p_rows, dtype=jnp.int32) - starts[row_expert]
    src = jnp.clip((jnp.cumsum(counts) - counts)[row_expert] + row_rank, 0, 2 * n - 1)
    row_token = jnp.where(row_rank < counts[row_expert], order[src] % n, 0)

    xs = x1b.reshape(n, d).at[row_token].get(mode='promise_in_bounds')
    ys = _experts_call(tile_expert, n_active, xs, w['mw1'], w['mw3'], w['mw2'], tm, tl['moe_chunks'])
    y2 = ys.at[dest].get(mode='promise_in_bounds').reshape(2, b, s, d)
    return _combine_call(x1, y2, gw, p, w, alpha, tl['tm'])


def _rope_tables(s):
    pos = jnp.arange(s, dtype=F32)
    inv_freq = ROPE_THETA ** (-jnp.arange(0, MLA_ROPE // 2, dtype=F32) * (2.0 / MLA_ROPE))
    ang = pos[:, None] * inv_freq[None, :]
    cos, sin = jnp.cos(ang), jnp.sin(ang)
    ck = jnp.concatenate([jnp.ones((s, MLA_NOPE), F32), cos, cos,
                          jnp.zeros((s, HEAD_PAD - MLA_NOPE - MLA_ROPE), F32)], axis=1)
    sk = jnp.concatenate([jnp.zeros((s, MLA_NOPE), F32), sin, sin,
                          jnp.zeros((s, HEAD_PAD - MLA_NOPE - MLA_ROPE), F32)], axis=1)
    qs = (MLA_NOPE + MLA_ROPE) ** -0.5 * math.log2(math.e)
    return (ck * qs).T, (sk * qs).T, ck, sk


def _rotate_half_cols(w_rope):
    half = w_rope.shape[-1] // 2
    return jnp.concatenate([-w_rope[..., half:], w_rope[..., :half]], axis=-1)


def _prep_layer(i, w_in, b_gate, q_norm_g, w_uq, kv_norm_g, w_ukv, w_na_o, w_mla_o, w_out, ln1_g, ln1_b,
                w_ple_gate, w_ple, ln2_g, ln2_b):
    d = w_in.shape[1]
    o = 0
    wna = w_in[i, :, o:o + 3 * NA_WIDTH]; o += 3 * NA_WIDTH
    wql = w_in[i, :, o:o + MLA_Q_LORA]; o += MLA_Q_LORA
    wkvl = w_in[i, :, o:o + MLA_KV_LORA]; o += MLA_KV_LORA
    wkr = w_in[i, :, o:o + MLA_ROPE]; o += MLA_ROPE
    wg = w_in[i, :, o:o + 2 * d]
    na_scale = jnp.concatenate([jnp.full((NA_WIDTH,), NA_HEAD_DIM ** -0.5, F32), jnp.ones((2 * NA_WIDTH,), F32)])
    wna = wna * na_scale[None, :]

    zpad = HEAD_PAD - MLA_NOPE - MLA_ROPE
    uq = w_uq[i].reshape(MLA_Q_LORA, MLA_HEADS, MLA_NOPE + MLA_ROPE)
    assert zpad == MLA_ROPE
    uq_pad = jnp.concatenate([uq, _rotate_half_cols(uq[..., MLA_NOPE:])], axis=-1)
    kr_pad = jnp.pad(wkr, ((0, 0), (MLA_NOPE, zpad)))
    kr_rot = jnp.pad(_rotate_half_cols(wkr), ((0, 0), (MLA_NOPE, zpad)))
    ukv = w_ukv[i].reshape(MLA_KV_LORA, MLA_HEADS, MLA_NOPE + MLA_V)
    uk_pad = jnp.pad(ukv[..., :MLA_NOPE], ((0, 0), (0, 0), (0, HEAD_PAD - MLA_NOPE)))
    uvt = ukv[..., MLA_NOPE:].reshape(MLA_KV_LORA, MLA_HEADS * MLA_V).T

    return dict(
        wna=wna.astype(BF), wg=wg.astype(BF), bg=b_gate[i][None, :], wql=wql.astype(BF), wkvl=wkvl.astype(BF),
        wkr=jnp.concatenate([kr_pad, kr_rot], axis=1).astype(BF),
        gq=q_norm_g[i][None, :], gkv=kv_norm_g[i][None, :],
        wuq=uq_pad.reshape(MLA_Q_LORA, MLA_HEADS * HEAD_PAD).T.astype(BF),
        wuk=uk_pad.reshape(MLA_KV_LORA, MLA_HEADS * HEAD_PAD).astype(BF), wuvt=uvt.astype(BF),
        wnao=w_na_o[i].astype(BF), wmlao=w_mla_o[i].astype(BF), wout=w_out[i].astype(BF),
        ln1g=ln1_g[i][None, :], ln1b=ln1_b[i][None, :],
        wpg=w_ple_gate[i].astype(BF), wp=w_ple[i].astype(BF), ln2g=ln2_g[i][None, :], ln2b=ln2_b[i][None, :],
    )


def kernel(x, p, w_in, b_gate, q_norm_g, w_uq, kv_norm_g, w_ukv, na_rpb, w_na_o, w_mla_o, w_out, ln1_g, ln1_b,
           ffn_w1, ffn_w3, ffn_w2, moe_w_router, moe_b_router, moe_w1, moe_w3, moe_w2, w_ple_gate, w_ple,
           ln2_g, ln2_b):
    b, s, d = x.shape
    depth = w_in.shape[0]
    alpha = (2 * depth) ** 0.25
    tl = _tiles(b * s, s)
    tabs = _rope_tables(s)

    for i in range(depth):
        w = _prep_layer(i, w_in, b_gate, q_norm_g, w_uq, kv_norm_g, w_ukv, w_na_o, w_mla_o, w_out, ln1_g, ln1_b,
                        w_ple_gate, w_ple, ln2_g, ln2_b)
        qkv, gates, q_mla, k_mla, vt = _proj_call(x, w, tabs, tl['tm'])
        o_na = _na_call(qkv, _na_bias_table(na_rpb[i]), tl['na_rows'])
        o_mla_t = _mla_call(q_mla, k_mla, vt, tl['tq'], tl['tk'], tl['mla_unroll'])
        o_mla_t = o_mla_t.reshape(b, MLA_HEADS * MLA_V, s)
        j = i // 2
        if i % 2 == 0:
            (x1,) = _merge_call(x, o_na, o_mla_t, gates, w, alpha, tl['tm'], route=False)
            w.update(w1=ffn_w1[j].astype(BF), w3=ffn_w3[j].astype(BF), w2=ffn_w2[j].astype(BF))
            x = _ffn_call(x1, p[i], w, alpha, tl['tm'], tl['ffn_chunks'])
        else:
            wr = jnp.pad(moe_w_router[j].T, ((0, BF16_SUBLANES - N_EXPERTS), (0, 0)))
            wrh = wr.astype(BF)
            br = jnp.concatenate([moe_b_router[j], jnp.full((BF16_SUBLANES - N_EXPERTS,), MASK_VALUE, F32)])
            w.update(wrh=wrh, wrl=(wr - wrh.astype(F32)).astype(BF), br=br[:, None],
                     mw1=moe_w1[j].astype(BF), mw3=moe_w3[j].astype(BF), mw2=moe_w2[j].astype(BF))
            x1, x1b, route = _merge_call(x, o_na, o_mla_t, gates, w, alpha, tl['tm'], route=True)
            x = _moe_layer(x1, x1b, route, p[i], w, alpha, tl)
    return x
```

```python
import functools
import math

import numpy as np
import jax
import jax.numpy as jnp
from jax import lax
from jax.experimental import pallas as pl
from jax.experimental.pallas import tpu as pltpu

BF = jnp.bfloat16
F32 = jnp.float32

GRID_W = 64
NA_HEADS = 8
NA_HEAD_DIM = 64
NA_ROWS = 8
NA_COLS = 16
NA_WIDTH = NA_HEADS * NA_HEAD_DIM
MLA_HEADS = 8
MLA_NOPE = 64
MLA_ROPE = 32
MLA_V = 64
MLA_Q_LORA = 768
MLA_KV_LORA = 256
ROPE_THETA = 10000.0
N_EXPERTS = 8
LN_EPS = 1e-5
RMS_EPS = 1e-6
MASK_VALUE = -1e30

LANES = 128
BF16_SUBLANES = 16
MXU_COLS = 256
VMEM_LIMIT_BYTES = 56 * 1024 * 1024

HEAD_PAD = LANES
VT_ROWS = MLA_V + BF16_SUBLANES


def _dot(a, b):
    return jnp.dot(a, b, preferred_element_type=F32)


def _dot_nt(a, b):
    return lax.dot_general(a, b, (((1,), (1,)), ((), ())), preferred_element_type=F32)


def _dot_tn(a, b):
    return lax.dot_general(a, b, (((0,), (0,)), ((), ())), preferred_element_type=F32)


def _const_spec(shape):
    nd = len(shape)
    return pl.BlockSpec(shape, lambda *_: (0,) * nd, pipeline_mode=pl.Buffered(1))


def _params(sem):
    return pltpu.CompilerParams(dimension_semantics=sem, vmem_limit_bytes=VMEM_LIMIT_BYTES)


def _tiles(n_tokens, seq):
    return dict(
        tm=min(512, seq),
        tq=min(1024, seq),
        tk=min(512, seq),
        mla_unroll=4 if (seq // min(512, seq)) % 4 == 0 else 2,
        na_rows=8 if seq // GRID_W >= 24 else 4,
        moe_tm=512 if n_tokens >= 8192 else 128,
        moe_chunks=2,
        ffn_chunks=1,
    )


def _layer_norm(y, g, b):
    mu = jnp.mean(y, axis=-1, keepdims=True)
    yc = y - mu
    var = jnp.mean(yc * yc, axis=-1, keepdims=True)
    return yc * lax.rsqrt(var + LN_EPS) * g + b


def _rms_norm(y, g):
    return y * lax.rsqrt(jnp.mean(y * y, axis=-1, keepdims=True) + RMS_EPS) * g


def _silu(y):
    return y * jax.nn.sigmoid(y)


def _proj_kernel(x_ref, wna_ref, wg_ref, bg_ref, wql_ref, wkvl_ref, wkr_ref, gq_ref, gkv_ref,
                 wuq_ref, wuk_ref, wuvt_ref, cq_ref, sq_ref, ck_ref, sk_ref,
                 qkv_ref, gate_ref, q_ref, k_ref, vt_ref):
    xb = x_ref[0].astype(BF)
    qkv_ref[0] = _dot(xb, wna_ref[...]).astype(BF)
    gate_ref[0] = jax.nn.sigmoid(_dot(xb, wg_ref[...]) + bg_ref[...]).astype(BF)

    qn = _rms_norm(_dot(xb, wql_ref[...]), gq_ref[...]).astype(BF)
    qat = _dot_nt(wuq_ref[...], qn)
    cq, sq = cq_ref[...], sq_ref[...]
    for h in range(MLA_HEADS):
        qh = qat[h * HEAD_PAD:(h + 1) * HEAD_PAD]
        partner = pltpu.roll(qh, shift=HEAD_PAD - MLA_ROPE, axis=0)
        q_ref[0, h] = (qh * cq + partner * sq).astype(BF)

    kvn = _rms_norm(_dot(xb, wkvl_ref[...]), gkv_ref[...]).astype(BF)
    kr2 = _dot(xb, wkr_ref[...])
    kr = kr2[:, :HEAD_PAD] * ck_ref[...] + kr2[:, HEAD_PAD:] * sk_ref[...]
    kn = _dot(kvn, wuk_ref[...])
    for h in range(MLA_HEADS):
        sl = slice(h * HEAD_PAD, (h + 1) * HEAD_PAD)
        k_ref[0, h] = (kn[:, sl] + kr).astype(BF)

    vt = _dot_nt(wuvt_ref[...], kvn)
    ones = jnp.ones((BF16_SUBLANES, vt.shape[1]), BF)
    for h in range(MLA_HEADS):
        vt_ref[0, h, 0:MLA_V, :] = vt[h * MLA_V:(h + 1) * MLA_V].astype(BF)
        vt_ref[0, h, MLA_V:VT_ROWS, :] = ones


def _proj_call(x, w, tabs, tm):
    b, s, d = x.shape
    grid = (b, s // tm)
    weights = [w['wna'], w['wg'], w['bg'], w['wql'], w['wkvl'], w['wkr'], w['gq'], w['gkv'],
               w['wuq'], w['wuk'], w['wuvt']]
    tab_spec = pl.BlockSpec((tm, HEAD_PAD), lambda i, j: (j, 0))
    tab_t_spec = pl.BlockSpec((HEAD_PAD, tm), lambda i, j: (0, j))
    in_specs = ([pl.BlockSpec((1, tm, d), lambda i, j: (i, j, 0))]
                + [_const_spec(a.shape) for a in weights] + [tab_t_spec] * 2 + [tab_spec] * 2)
    out_shape = (
        jax.ShapeDtypeStruct((b, s, 3 * NA_WIDTH), BF),
        jax.ShapeDtypeStruct((b, s, 2 * d), BF),
        jax.ShapeDtypeStruct((b, MLA_HEADS, HEAD_PAD, s), BF),
        jax.ShapeDtypeStruct((b, MLA_HEADS, s, HEAD_PAD), BF),
        jax.ShapeDtypeStruct((b, MLA_HEADS, VT_ROWS, s), BF),
    )
    out_specs = (
        pl.BlockSpec((1, tm, 3 * NA_WIDTH), lambda i, j: (i, j, 0)),
        pl.BlockSpec((1, tm, 2 * d), lambda i, j: (i, j, 0)),
        pl.BlockSpec((1, MLA_HEADS, HEAD_PAD, tm), lambda i, j: (i, 0, 0, j)),
        pl.BlockSpec((1, MLA_HEADS, tm, HEAD_PAD), lambda i, j: (i, 0, j, 0)),
        pl.BlockSpec((1, MLA_HEADS, VT_ROWS, tm), lambda i, j: (i, 0, 0, j)),
    )
    return pl.pallas_call(
        _proj_kernel, grid=grid, in_specs=in_specs, out_specs=out_specs, out_shape=out_shape,
        compiler_params=_params(("parallel", "parallel")), name="proj",
    )(x, *weights, *tabs)


def _na_halo_start(i, rows, rb):
    return jnp.clip((i - 1) * rb, 0, rows - 3 * rb)


def _na_kernel(q_ref, kwin_ref, vwin_ref, bias_ref, o_ref, *, rows, rb):
    i = pl.program_id(1)
    win = NA_ROWS * GRID_W
    base = _na_halo_start(i, rows, rb)
    kcat_ref, vcat_ref = kwin_ref.at[0], vwin_ref.at[0]
    lane = lax.broadcasted_iota(jnp.int32, (GRID_W, LANES), 1)
    first_head = lane < NA_HEAD_DIM

    def row(a, carry):
        r = i * rb + a
        rs = jnp.clip(r - NA_ROWS // 2, 0, rows - NA_ROWS)
        off = pl.multiple_of((rs - base) * GRID_W, GRID_W)
        d = rs - r + (NA_ROWS - 1)
        qoff = pl.multiple_of(a * GRID_W, GRID_W)
        qrow = q_ref[0, pl.ds(qoff, GRID_W), :]
        zero = jnp.zeros((GRID_W, LANES), BF)
        sc = []
        for hp in range(NA_HEADS // 2):
            sl = slice(hp * LANES, (hp + 1) * LANES)
            qp = qrow[:, sl]
            qm = jnp.concatenate([jnp.where(first_head, qp, zero), jnp.where(first_head, zero, qp)], axis=0)
            sc.append(_dot_nt(qm, kcat_ref[pl.ds(off, win), sl]))
        sc = jnp.concatenate(sc, axis=0) + bias_ref[d]
        m = jnp.max(sc, axis=-1, keepdims=True)
        p = jnp.exp(sc - m)
        inv_l = 1.0 / jnp.sum(p, axis=-1, keepdims=True)
        pb = p.astype(BF)
        for hp in range(NA_HEADS // 2):
            sl = slice(hp * LANES, (hp + 1) * LANES)
            rows2 = slice(hp * 2 * GRID_W, (hp + 1) * 2 * GRID_W)
            o2 = _dot(pb[rows2], vcat_ref[pl.ds(off, win), sl]) * inv_l[rows2]
            o_ref[0, pl.ds(qoff, GRID_W), sl] = jnp.where(first_head, o2[:GRID_W], o2[GRID_W:]).astype(BF)
        return carry

    lax.fori_loop(0, rb, row, 0, unroll=True)


def _na_call(qkv, bias_tab, rb):
    b, s, _ = qkv.shape
    rows = s // GRID_W
    nb = rows // rb
    blk = rb * GRID_W

    assert rb >= NA_ROWS // 2 and rows >= 3 * rb and rows % rb == 0

    def halo_spec(col):
        return pl.BlockSpec((pl.Element(1), pl.Element(3 * blk), pl.Element(NA_WIDTH)),
                            lambda i, j: (i, _na_halo_start(j, rows, rb) * GRID_W, col * NA_WIDTH))

    in_specs = [pl.BlockSpec((1, blk, NA_WIDTH), lambda i, j: (i, j, 0)),
                halo_spec(1), halo_spec(2), _const_spec(bias_tab.shape)]
    return pl.pallas_call(
        functools.partial(_na_kernel, rows=rows, rb=rb),
        grid=(b, nb), in_specs=in_specs,
        out_specs=pl.BlockSpec((1, blk, NA_WIDTH), lambda i, j: (i, j, 0)),
        out_shape=jax.ShapeDtypeStruct((b, s, NA_WIDTH), BF),
        compiler_params=_params(("parallel", "parallel")), name="na_attn",
    )(qkv, qkv, qkv, bias_tab)


def _na_bias_table(rpb):
    qc = np.arange(GRID_W)[:, None]
    kc = np.arange(GRID_W)[None, :]
    win_start = np.clip(qc - NA_COLS // 2, 0, GRID_W - NA_COLS)
    valid = (kc >= win_start) & (kc < win_start + NA_COLS)
    dc = np.clip(kc - qc + NA_COLS - 1, 0, 2 * NA_COLS - 2)
    tab = jnp.where(jnp.asarray(valid)[None, None], rpb[:, :, dc], MASK_VALUE)
    t = jnp.stack([tab[:, d:d + NA_ROWS] for d in range(NA_ROWS)], axis=0)
    t = t.transpose(0, 1, 3, 2, 4)
    return t.reshape(NA_ROWS, NA_HEADS * GRID_W, NA_ROWS * GRID_W).astype(F32)


def _mla_kernel(q_ref, k_ref, vt_ref, o_ref, s0_ref, s1_ref, mx_ref, acc_ref, m_ref, *, tk, unroll):
    qt = q_ref[0, 0]
    tq = qt.shape[1]
    nk = k_ref.shape[2] // tk
    s_refs = (s0_ref, s1_ref)

    def key_slice(c):
        return pl.ds(pl.multiple_of(c * tk, tk), tk)

    def scores(c, slot):
        st = _dot(k_ref[0, 0, key_slice(c), :], qt)
        s_refs[slot][...] = st
        mx_ref[slot] = jnp.max(st, axis=0, keepdims=True)

    def consume(c, slot):
        s_ref = s_refs[slot]
        vt = vt_ref[0, 0, :, key_slice(c)]
        for j in range(tq // MXU_COLS):
            cs = slice(j * MXU_COLS, (j + 1) * MXU_COLS)
            m = m_ref[:, cs]
            m_new = jnp.maximum(m, mx_ref[slot, :, cs])
            p = jnp.exp2(s_ref[:, cs] - m_new).astype(BF)
            acc_ref[:, cs] = acc_ref[:, cs] * jnp.exp2(m - m_new) + _dot(vt, p)
            m_ref[:, cs] = m_new

    scores(0, 0)
    m_ref[...] = jnp.full(m_ref.shape, MASK_VALUE, F32)
    acc_ref[...] = jnp.zeros(acc_ref.shape, F32)

    def body(i, carry):
        c0 = unroll * i
        for u in range(unroll):
            scores(c0 + u + 1, (u + 1) % 2)
            consume(c0 + u, u % 2)
        return carry

    lax.fori_loop(0, nk // unroll - 1, body, 0)
    c0 = nk - unroll
    for u in range(unroll):
        if u + 1 < unroll:
            scores(c0 + u + 1, (u + 1) % 2)
        consume(c0 + u, u % 2)
    acc = acc_ref[...]
    o_ref[0, 0] = (acc[0:MLA_V] / acc[MLA_V:MLA_V + 1]).astype(BF)


def _mla_call(q, k, vt, tq, tk, unroll):
    b, h, s, _ = k.shape
    assert unroll % 2 == 0 and (s // tk) % unroll == 0
    return pl.pallas_call(
        functools.partial(_mla_kernel, tk=tk, unroll=unroll),
        grid=(b, h, s // tq),
        in_specs=[pl.BlockSpec((1, 1, HEAD_PAD, tq), lambda i, j, t: (i, j, 0, t)),
                  pl.BlockSpec((1, 1, s, HEAD_PAD), lambda i, j, t: (i, j, 0, 0)),
                  pl.BlockSpec((1, 1, VT_ROWS, s), lambda i, j, t: (i, j, 0, 0))],
        out_specs=pl.BlockSpec((1, 1, MLA_V, tq), lambda i, j, t: (i, j, 0, t)),
        out_shape=jax.ShapeDtypeStruct((b, h, MLA_V, s), BF),
        scratch_shapes=[pltpu.VMEM((tk, tq), F32), pltpu.VMEM((tk, tq), F32), pltpu.VMEM((2, 1, tq), F32),
                        pltpu.VMEM((VT_ROWS, tq), F32), pltpu.VMEM((1, tq), F32)],
        compiler_params=_params(("parallel", "parallel", "parallel")), name="mla_attn",
    )(q, k, vt)


def _top2_route(x, wh, wl, bias):
    xh = x.astype(BF)
    xl = (x - xh.astype(F32)).astype(BF)
    lg = _dot_nt(wh, xh) + _dot_nt(wh, xl) + _dot_nt(wl, xh) + bias
    rid = lax.broadcasted_iota(jnp.int32, lg.shape, 0)
    pad = lg.shape[0]
    m1 = jnp.max(lg, axis=0, keepdims=True)
    i1 = jnp.min(jnp.where(lg == m1, rid, pad), axis=0, keepdims=True)
    lg2 = jnp.where(rid == i1, MASK_VALUE, lg)
    m2 = jnp.max(lg2, axis=0, keepdims=True)
    i2 = jnp.min(jnp.where(lg2 == m2, rid, pad), axis=0, keepdims=True)
    e = jnp.exp(m2 - m1)
    return jnp.concatenate([i1.astype(F32), i2.astype(F32), 1.0 / (1.0 + e), e / (1.0 + e),
                            jnp.zeros((4, lg.shape[1]), F32)], axis=0)


def _merge_kernel(x_ref, ona_ref, omt_ref, gate_ref, wna_ref, wmla_ref, wout_ref, g_ref, b_ref, *rest,
                  alpha, route):
    d = x_ref.shape[2]
    a = _dot(ona_ref[0], wna_ref[...])
    m = _dot_tn(omt_ref[0], wmla_ref[...])
    gate = gate_ref[0]
    merged = gate[:, :d].astype(F32) * a + gate[:, d:].astype(F32) * m
    y = alpha * x_ref[0] + _dot(merged.astype(BF), wout_ref[...])
    out = _layer_norm(y, g_ref[...], b_ref[...])
    if route:
        wh_ref, wl_ref, br_ref, o_ref, ob_ref, r_ref = rest
        ob_ref[0] = out.astype(BF)
        r_ref[...] = _top2_route(out, wh_ref[...], wl_ref[...], br_ref[...])
    else:
        (o_ref,) = rest
    o_ref[0] = out


def _merge_call(x, o_na, o_mla_t, gates, w, alpha, tm, route):
    b, s, d = x.shape
    nt = s // tm
    weights = [w['wnao'], w['wmlao'], w['wout'], w['ln1g'], w['ln1b']]
    tok = pl.BlockSpec((1, tm, d), lambda i, j: (i, j, 0))
    out_specs, out_shape = [tok], [jax.ShapeDtypeStruct((b, s, d), F32)]
    if route:
        weights += [w['wrh'], w['wrl'], w['br']]
        out_specs += [tok, pl.BlockSpec((8, tm), lambda i, j: (0, i * nt + j))]
        out_shape += [jax.ShapeDtypeStruct((b, s, d), BF), jax.ShapeDtypeStruct((8, b * s), F32)]
    in_specs = [tok,
                pl.BlockSpec((1, tm, NA_WIDTH), lambda i, j: (i, j, 0)),
                pl.BlockSpec((1, MLA_HEADS * MLA_V, tm), lambda i, j: (i, 0, j)),
                pl.BlockSpec((1, tm, 2 * d), lambda i, j: (i, j, 0))] + [_const_spec(a.shape) for a in weights]
    return pl.pallas_call(
        functools.partial(_merge_kernel, alpha=alpha, route=route),
        grid=(b, nt), in_specs=in_specs, out_specs=tuple(out_specs), out_shape=tuple(out_shape),
        compiler_params=_params(("parallel", "parallel")), name="merge_ln1",
    )(x, o_na, o_mla_t, gates, *weights)


def _ple_ln2(x, xb, f, p_ref, wpg_ref, wp_ref, g_ref, b_ref, alpha):
    ple = jax.nn.sigmoid(_dot(xb, wpg_ref[...])) * _dot(p_ref[0].astype(BF), wp_ref[...])
    return _layer_norm(alpha * x + f + ple, g_ref[...], b_ref[...])


def _ffn_kernel(x_ref, p_ref, w1_ref, w3_ref, w2_ref, wpg_ref, wp_ref, g_ref, b_ref, o_ref,
                *, alpha, chunks):
    x = x_ref[0]
    xb = x.astype(BF)
    fc = w1_ref.shape[1] // chunks
    f = None
    for c in range(chunks):
        sl = slice(c * fc, (c + 1) * fc)
        h = _silu(_dot(xb, w1_ref[:, sl])) * _dot(xb, w3_ref[:, sl])
        part = _dot(h.astype(BF), w2_ref[sl, :])
        f = part if f is None else f + part
    o_ref[0] = _ple_ln2(x, xb, f, p_ref, wpg_ref, wp_ref, g_ref, b_ref, alpha)


def _ffn_call(x, p, w, alpha, tm, chunks):
    b, s, d = x.shape
    weights = [w['w1'], w['w3'], w['w2'], w['wpg'], w['wp'], w['ln2g'], w['ln2b']]
    in_specs = [pl.BlockSpec((1, tm, d), lambda i, j: (i, j, 0)),
                pl.BlockSpec((1, tm, p.shape[-1]), lambda i, j: (i, j, 0))] + [_const_spec(a.shape) for a in weights]
    return pl.pallas_call(
        functools.partial(_ffn_kernel, alpha=alpha, chunks=chunks),
        grid=(b, s // tm), in_specs=in_specs,
        out_specs=pl.BlockSpec((1, tm, d), lambda i, j: (i, j, 0)),
        out_shape=jax.ShapeDtypeStruct((b, s, d), F32),
        compiler_params=_params(("parallel", "parallel")), name="ffn_ple_ln2",
    )(x, p, *weights)


def _experts_kernel(te_ref, na_ref, xs_ref, w1_ref, w3_ref, w2_ref, o_ref, *, chunks):
    active = pl.program_id(0) < na_ref[0]

    @pl.when(active)
    def _():
        xb = xs_ref[...]
        fc = w1_ref.shape[2] // chunks
        acc = None
        for c in range(chunks):
            sl = slice(c * fc, (c + 1) * fc)
            h = _silu(_dot(xb, w1_ref[0, :, sl])) * _dot(xb, w3_ref[0, :, sl])
            part = _dot(h.astype(BF), w2_ref[0, sl, :])
            acc = part if acc is None else acc + part
        o_ref[...] = acc.astype(BF)

    @pl.when(jnp.logical_not(active))
    def _():
        o_ref[...] = jnp.zeros_like(o_ref)


def _experts_call(tile_expert, n_active, xs, w1, w3, w2, tm, chunks):
    p_rows, d = xs.shape
    dff = w1.shape[2]

    def expert_spec(shape):
        return pl.BlockSpec(shape, lambda t, te, na: (te[t], 0, 0), pipeline_mode=pl.Buffered(1))

    grid_spec = pltpu.PrefetchScalarGridSpec(
        num_scalar_prefetch=2, grid=(p_rows // tm,),
        in_specs=[pl.BlockSpec((tm, d), lambda t, te, na: (t, 0)),
                  expert_spec((1, d, dff)), expert_spec((1, d, dff)), expert_spec((1, dff, d))],
        out_specs=pl.BlockSpec((tm, d), lambda t, te, na: (t, 0)))
    return pl.pallas_call(
        functools.partial(_experts_kernel, chunks=chunks), grid_spec=grid_spec,
        out_shape=jax.ShapeDtypeStruct((p_rows, d), BF),
        compiler_params=_params(("arbitrary",)), name="experts",
    )(tile_expert, n_active, xs, w1, w3, w2)


def _combine_kernel(x_ref, ya_ref, yb_ref, gw_ref, p_ref, wpg_ref, wp_ref, g_ref, b_ref, o_ref, *, alpha):
    x = x_ref[0]
    gw = gw_ref[0]
    f = gw[:, 0:1] * ya_ref[0, 0].astype(F32) + gw[:, 1:2] * yb_ref[0, 0].astype(F32)
    o_ref[0] = _ple_ln2(x, x.astype(BF), f, p_ref, wpg_ref, wp_ref, g_ref, b_ref, alpha)


def _combine_call(x, y2, gw, p, w, alpha, tm):
    b, s, d = x.shape
    weights = [w['wpg'], w['wp'], w['ln2g'], w['ln2b']]
    tok = lambda width: pl.BlockSpec((1, tm, width), lambda i, j: (i, j, 0))
    choice = lambda k: pl.BlockSpec((1, 1, tm, d), lambda i, j: (k, i, j, 0))
    in_specs = ([tok(d), choice(0), choice(1), tok(2), tok(p.shape[-1])]
                + [_const_spec(a.shape) for a in weights])
    return pl.pallas_call(
        functools.partial(_combine_kernel, alpha=alpha),
        grid=(b, s // tm), in_specs=in_specs, out_specs=tok(d),
        out_shape=jax.ShapeDtypeStruct((b, s, d), F32),
        compiler_params=_params(("parallel", "parallel")), name="moe_combine_ple_ln2",
    )(x, y2, y2, gw, p, *weights)


def _moe_layer(x1, x1b, route, p, w, alpha, tl):
    b, s, d = x1.shape
    n = b * s
    tm = tl['moe_tm']
    e_all = route[0:2].astype(jnp.int32).reshape(2 * n)
    gw = route[2:4].T.reshape(b, s, 2)

    onehot = (e_all[:, None] == jnp.arange(N_EXPERTS, dtype=jnp.int32)[None, :]).astype(jnp.int32)
    csum = jnp.cumsum(onehot, axis=0)
    rank = jnp.sum(onehot * (csum - 1), axis=1)
    counts = csum[-1]
    padded = ((counts + tm - 1) // tm) * tm
    ends = jnp.cumsum(padded)
    starts = ends - padded
    dest = starts[e_all] + rank
    p_rows = 2 * n + N_EXPERTS * tm
    n_active = (ends[-1] // tm).astype(jnp.int32).reshape(1)
    tile_start = jnp.minimum(jnp.arange(p_rows // tm, dtype=jnp.int32), n_active[0] - 1) * tm
    tile_expert = jnp.sum((tile_start[:, None] >= ends[None, :]).astype(jnp.int32), axis=1)
    tile_expert = jnp.minimum(tile_expert, N_EXPERTS - 1).astype(jnp.int32)
    order = jnp.argsort(e_all, stable=True).astype(jnp.int32)
    row_expert = jnp.repeat(tile_expert, tm)
    row_rank = jnp.arange(p_rows, dtype=jnp.int32) - starts[row_expert]
    src = jnp.clip((jnp.cumsum(counts) - counts)[row_expert] + row_rank, 0, 2 * n - 1)
    row_token = jnp.where(row_rank < counts[row_expert], order[src] % n, 0)

    xs = x1b.reshape(n, d).at[row_token].get(mode='promise_in_bounds')
    ys = _experts_call(tile_expert, n_active, xs, w['mw1'], w['mw3'], w['mw2'], tm, tl['moe_chunks'])
    y2 = ys.at[dest].get(mode='promise_in_bounds').reshape(2, b, s, d)
    return _combine_call(x1, y2, gw, p, w, alpha, tl['tm'])


def _rope_tables(s):
    pos = jnp.arange(s, dtype=F32)
    inv_freq = ROPE_THETA ** (-jnp.arange(0, MLA_ROPE // 2, dtype=F32) * (2.0 / MLA_ROPE))
    ang = pos[:, None] * inv_freq[None, :]
    cos, sin = jnp.cos(ang), jnp.sin(ang)
    ck = jnp.concatenate([jnp.ones((s, MLA_NOPE), F32), cos, cos,
                          jnp.zeros((s, HEAD_PAD - MLA_NOPE - MLA_ROPE), F32)], axis=1)
    sk = jnp.concatenate([jnp.zeros((s, MLA_NOPE), F32), sin, sin,
                          jnp.zeros((s, HEAD_PAD - MLA_NOPE - MLA_ROPE), F32)], axis=1)
    qs = (MLA_NOPE + MLA_ROPE) ** -0.5 * math.log2(math.e)
    return (ck * qs).T, (sk * qs).T, ck, sk


def _rotate_half_cols(w_rope):
    half = w_rope.shape[-1] // 2
    return jnp.concatenate([-w_rope[..., half:], w_rope[..., :half]], axis=-1)


def _prep_layer(i, w_in, b_gate, q_norm_g, w_uq, kv_norm_g, w_ukv, w_na_o, w_mla_o, w_out, ln1_g, ln1_b,
                w_ple_gate, w_ple, ln2_g, ln2_b):
    d = w_in.shape[1]
    o = 0
    wna = w_in[i, :, o:o + 3 * NA_WIDTH]; o += 3 * NA_WIDTH
    wql = w_in[i, :, o:o + MLA_Q_LORA]; o += MLA_Q_LORA
    wkvl = w_in[i, :, o:o + MLA_KV_LORA]; o += MLA_KV_LORA
    wkr = w_in[i, :, o:o + MLA_ROPE]; o += MLA_ROPE
    wg = w_in[i, :, o:o + 2 * d]
    na_scale = jnp.concatenate([jnp.full((NA_WIDTH,), NA_HEAD_DIM ** -0.5, F32), jnp.ones((2 * NA_WIDTH,), F32)])
    wna = wna * na_scale[None, :]

    zpad = HEAD_PAD - MLA_NOPE - MLA_ROPE
    uq = w_uq[i].reshape(MLA_Q_LORA, MLA_HEADS, MLA_NOPE + MLA_ROPE)
    assert zpad == MLA_ROPE
    uq_pad = jnp.concatenate([uq, _rotate_half_cols(uq[..., MLA_NOPE:])], axis=-1)
    kr_pad = jnp.pad(wkr, ((0, 0), (MLA_NOPE, zpad)))
    kr_rot = jnp.pad(_rotate_half_cols(wkr), ((0, 0), (MLA_NOPE, zpad)))
    ukv = w_ukv[i].reshape(MLA_KV_LORA, MLA_HEADS, MLA_NOPE + MLA_V)
    uk_pad = jnp.pad(ukv[..., :MLA_NOPE], ((0, 0), (0, 0), (0, HEAD_PAD - MLA_NOPE)))
    uvt = ukv[..., MLA_NOPE:].reshape(MLA_KV_LORA, MLA_HEADS * MLA_V).T

    return dict(
        wna=wna.astype(BF), wg=wg.astype(BF), bg=b_gate[i][None, :], wql=wql.astype(BF), wkvl=wkvl.astype(BF),
        wkr=jnp.concatenate([kr_pad, kr_rot], axis=1).astype(BF),
        gq=q_norm_g[i][None, :], gkv=kv_norm_g[i][None, :],
        wuq=uq_pad.reshape(MLA_Q_LORA, MLA_HEADS * HEAD_PAD).T.astype(BF),
        wuk=uk_pad.reshape(MLA_KV_LORA, MLA_HEADS * HEAD_PAD).astype(BF), wuvt=uvt.astype(BF),
        wnao=w_na_o[i].astype(BF), wmlao=w_mla_o[i].astype(BF), wout=w_out[i].astype(BF),
        ln1g=ln1_g[i][None, :], ln1b=ln1_b[i][None, :],
        wpg=w_ple_gate[i].astype(BF), wp=w_ple[i].astype(BF), ln2g=ln2_g[i][None, :], ln2b=ln2_b[i][None, :],
    )


def kernel(x, p, w_in, b_gate, q_norm_g, w_uq, kv_norm_g, w_ukv, na_rpb, w_na_o, w_mla_o, w_out, ln1_g, ln1_b,
           ffn_w1, ffn_w3, ffn_w2, moe_w_router, moe_b_router, moe_w1, moe_w3, moe_w2, w_ple_gate, w_ple,
           ln2_g, ln2_b):
    b, s, d = x.shape
    depth = w_in.shape[0]
    alpha = (2 * depth) ** 0.25
    tl = _tiles(b * s, s)
    tabs = _rope_tables(s)

    for i in range(depth):
        w = _prep_layer(i, w_in, b_gate, q_norm_g, w_uq, kv_norm_g, w_ukv, w_na_o, w_mla_o, w_out, ln1_g, ln1_b,
                        w_ple_gate, w_ple, ln2_g, ln2_b)
        qkv, gates, q_mla, k_mla, vt = _proj_call(x, w, tabs, tl['tm'])
        o_na = _na_call(qkv, _na_bias_table(na_rpb[i]), tl['na_rows'])
        o_mla_t = _mla_call(q_mla, k_mla, vt, tl['tq'], tl['tk'], tl['mla_unroll'])
        o_mla_t = o_mla_t.reshape(b, MLA_HEADS * MLA_V, s)
        j = i // 2
        if i % 2 == 0:
            (x1,) = _merge_call(x, o_na, o_mla_t, gates, w, alpha, tl['tm'], route=False)
            w.update(w1=ffn_w1[j].astype(BF), w3=ffn_w3[j].astype(BF), w2=ffn_w2[j].astype(BF))
            x = _ffn_call(x1, p[i], w, alpha, tl['tm'], tl['ffn_chunks'])
        else:
            wr = jnp.pad(moe_w_router[j].T, ((0, BF16_SUBLANES - N_EXPERTS), (0, 0)))
            wrh = wr.astype(BF)
            br = jnp.concatenate([moe_b_router[j], jnp.full((BF16_SUBLANES - N_EXPERTS,), MASK_VALUE, F32)])
            w.update(wrh=wrh, wrl=(wr - wrh.astype(F32)).astype(BF), br=br[:, None],
                     mw1=moe_w1[j].astype(BF), mw3=moe_w3[j].astype(BF), mw2=moe_w2[j].astype(BF))
            x1, x1b, route = _merge_call(x, o_na, o_mla_t, gates, w, alpha, tl['tm'], route=True)
            x = _moe_layer(x1, x1b, route, p[i], w, alpha, tl)
    return x
```

```python
import functools
import math

import numpy as np
import jax
import jax.numpy as jnp
from jax import lax
from jax.experimental import pallas as pl
from jax.experimental.pallas import tpu as pltpu

BF = jnp.bfloat16
F32 = jnp.float32

GRID_W = 64
NA_HEADS = 8
NA_HEAD_DIM = 64
NA_ROWS = 8
NA_COLS = 16
NA_WIDTH = NA_HEADS * NA_HEAD_DIM
MLA_HEADS = 8
MLA_NOPE = 64
MLA_ROPE = 32
MLA_V = 64
MLA_Q_LORA = 768
MLA_KV_LORA = 256
ROPE_THETA = 10000.0
N_EXPERTS = 8
LN_EPS = 1e-5
RMS_EPS = 1e-6
MASK_VALUE = -1e30

LANES = 128
BF16_SUBLANES = 16
MXU_COLS = 256
VMEM_LIMIT_BYTES = 56 * 1024 * 1024

HEAD_PAD = LANES
VT_ROWS = MLA_V + BF16_SUBLANES


def _dot(a, b):
    return jnp.dot(a, b, preferred_element_type=F32)


def _dot_nt(a, b):
    return lax.dot_general(a, b, (((1,), (1,)), ((), ())), preferred_element_type=F32)


def _dot_tn(a, b):
    return lax.dot_general(a, b, (((0,), (0,)), ((), ())), preferred_element_type=F32)


def _const_spec(shape):
    nd = len(shape)
    return pl.BlockSpec(shape, lambda *_: (0,) * nd, pipeline_mode=pl.Buffered(1))


def _params(sem):
    return pltpu.CompilerParams(dimension_semantics=sem, vmem_limit_bytes=VMEM_LIMIT_BYTES)


def _tiles(n_tokens, seq):
    return dict(
        tm=min(512, seq),
        tq=min(1024, seq),
        tk=min(512, seq),
        mla_unroll=4 if (seq // min(512, seq)) % 4 == 0 else 2,
        na_rows=8 if seq // GRID_W >= 24 else 4,
        moe_tm=512 if n_tokens >= 8192 else 128,
        moe_chunks=2,
        ffn_chunks=1,
    )


def _layer_norm(y, g, b):
    mu = jnp.mean(y, axis=-1, keepdims=True)
    yc = y - mu
    var = jnp.mean(yc * yc, axis=-1, keepdims=True)
    return yc * lax.rsqrt(var + LN_EPS) * g + b


def _rms_norm(y, g):
    return y * lax.rsqrt(jnp.mean(y * y, axis=-1, keepdims=True) + RMS_EPS) * g


def _silu(y):
    return y * jax.nn.sigmoid(y)


def _proj_kernel(x_ref, wna_ref, wg_ref, bg_ref, wql_ref, wkvl_ref, wkr_ref, gq_ref, gkv_ref,
                 wuq_ref, wuk_ref, wuvt_ref, cq_ref, sq_ref, ck_ref, sk_ref,
                 qkv_ref, gate_ref, q_ref, k_ref, vt_ref):
    xb = x_ref[0].astype(BF)
    qkv_ref[0] = _dot(xb, wna_ref[...]).astype(BF)
    gate_ref[0] = jax.nn.sigmoid(_dot(xb, wg_ref[...]) + bg_ref[...]).astype(BF)

    qn = _rms_norm(_dot(xb, wql_ref[...]), gq_ref[...]).astype(BF)
    qat = _dot_nt(wuq_ref[...], qn)
    cq, sq = cq_ref[...], sq_ref[...]
    for h in range(MLA_HEADS):
        qh = qat[h * HEAD_PAD:(h + 1) * HEAD_PAD]
        partner = pltpu.roll(qh, shift=HEAD_PAD - MLA_ROPE, axis=0)
        q_ref[0, h] = (qh * cq + partner * sq).astype(BF)

    kvn = _rms_norm(_dot(xb, wkvl_ref[...]), gkv_ref[...]).astype(BF)
    kr2 = _dot(xb, wkr_ref[...])
    kr = kr2[:, :HEAD_PAD] * ck_ref[...] + kr2[:, HEAD_PAD:] * sk_ref[...]
    kn = _dot(kvn, wuk_ref[...])
    for h in range(MLA_HEADS):
        sl = slice(h * HEAD_PAD, (h + 1) * HEAD_PAD)
        k_ref[0, h] = (kn[:, sl] + kr).astype(BF)

    vt = _dot_nt(wuvt_ref[...], kvn)
    ones = jnp.ones((BF16_SUBLANES, vt.shape[1]), BF)
    for h in range(MLA_HEADS):
        vt_ref[0, h, 0:MLA_V, :] = vt[h * MLA_V:(h + 1) * MLA_V].astype(BF)
        vt_ref[0, h, MLA_V:VT_ROWS, :] = ones


def _proj_call(x, w, tabs, tm):
    b, s, d = x.shape
    grid = (b, s // tm)
    weights = [w['wna'], w['wg'], w['bg'], w['wql'], w['wkvl'], w['wkr'], w['gq'], w['gkv'],
               w['wuq'], w['wuk'], w['wuvt']]
    tab_spec = pl.BlockSpec((tm, HEAD_PAD), lambda i, j: (j, 0))
    tab_t_spec = pl.BlockSpec((HEAD_PAD, tm), lambda i, j: (0, j))
    in_specs = ([pl.BlockSpec((1, tm, d), lambda i, j: (i, j, 0))]
                + [_const_spec(a.shape) for a in weights] + [tab_t_spec] * 2 + [tab_spec] * 2)
    out_shape = (
        jax.ShapeDtypeStruct((b, s, 3 * NA_WIDTH), BF),
        jax.ShapeDtypeStruct((b, s, 2 * d), BF),
        jax.ShapeDtypeStruct((b, MLA_HEADS, HEAD_PAD, s), BF),
        jax.ShapeDtypeStruct((b, MLA_HEADS, s, HEAD_PAD), BF),
        jax.ShapeDtypeStruct((b, MLA_HEADS, VT_ROWS, s), BF),
    )
    out_specs = (
        pl.BlockSpec((1, tm, 3 * NA_WIDTH), lambda i, j: (i, j, 0)),
        pl.BlockSpec((1, tm, 2 * d), lambda i, j: (i, j, 0)),
        pl.BlockSpec((1, MLA_HEADS, HEAD_PAD, tm), lambda i, j: (i, 0, 0, j)),
        pl.BlockSpec((1, MLA_HEADS, tm, HEAD_PAD), lambda i, j: (i, 0, j, 0)),
        pl.BlockSpec((1, MLA_HEADS, VT_ROWS, tm), lambda i, j: (i, 0, 0, j)),
    )
    return pl.pallas_call(
        _proj_kernel, grid=grid, in_specs=in_specs, out_specs=out_specs, out_shape=out_shape,
        compiler_params=_params(("parallel", "parallel")), name="proj",
    )(x, *weights, *tabs)


def _na_halo_start(i, rows, rb):
    return jnp.clip((i - 1) * rb, 0, rows - 3 * rb)


def _na_kernel(q_ref, kwin_ref, vwin_ref, bias_ref, o_ref, *, rows, rb):
    i = pl.program_id(1)
    win = NA_ROWS * GRID_W
    base = _na_halo_start(i, rows, rb)
    kcat_ref, vcat_ref = kwin_ref.at[0], vwin_ref.at[0]
    lane = lax.broadcasted_iota(jnp.int32, (GRID_W, LANES), 1)
    first_head = lane < NA_HEAD_DIM

    def row(a, carry):
        r = i * rb + a
        rs = jnp.clip(r - NA_ROWS // 2, 0, rows - NA_ROWS)
        off = pl.multiple_of((rs - base) * GRID_W, GRID_W)
        d = rs - r + (NA_ROWS - 1)
        qoff = pl.multiple_of(a * GRID_W, GRID_W)
        qrow = q_ref[0, pl.ds(qoff, GRID_W), :]
        zero = jnp.zeros((GRID_W, LANES), BF)
        sc = []
        for hp in range(NA_HEADS // 2):
            sl = slice(hp * LANES, (hp + 1) * LANES)
            qp = qrow[:, sl]
            qm = jnp.concatenate([jnp.where(first_head, qp, zero), jnp.where(first_head, zero, qp)], axis=0)
            sc.append(_dot_nt(qm, kcat_ref[pl.ds(off, win), sl]))
        sc = jnp.concatenate(sc, axis=0) + bias_ref[d]
        m = jnp.max(sc, axis=-1, keepdims=True)
        p = jnp.exp(sc - m)
        inv_l = 1.0 / jnp.sum(p, axis=-1, keepdims=True)
        pb = p.astype(BF)
        for hp in range(NA_HEADS // 2):
            sl = slice(hp * LANES, (hp + 1) * LANES)
            rows2 = slice(hp * 2 * GRID_W, (hp + 1) * 2 * GRID_W)
            o2 = _dot(pb[rows2], vcat_ref[pl.ds(off, win), sl]) * inv_l[rows2]
            o_ref[0, pl.ds(qoff, GRID_W), sl] = jnp.where(first_head, o2[:GRID_W], o2[GRID_W:]).astype(BF)
        return carry

    lax.fori_loop(0, rb, row, 0, unroll=True)


def _na_call(qkv, bias_tab, rb):
    b, s, _ = qkv.shape
    rows = s // GRID_W
    nb = rows // rb
    blk = rb * GRID_W

    assert rb >= NA_ROWS // 2 and rows >= 3 * rb and rows % rb == 0

    def halo_spec(col):
        return pl.BlockSpec((pl.Element(1), pl.Element(3 * blk), pl.Element(NA_WIDTH)),
                            lambda i, j: (i, _na_halo_start(j, rows, rb) * GRID_W, col * NA_WIDTH))

    in_specs = [pl.BlockSpec((1, blk, NA_WIDTH), lambda i, j: (i, j, 0)),
                halo_spec(1), halo_spec(2), _const_spec(bias_tab.shape)]
    return pl.pallas_call(
        functools.partial(_na_kernel, rows=rows, rb=rb),
        grid=(b, nb), in_specs=in_specs,
        out_specs=pl.BlockSpec((1, blk, NA_WIDTH), lambda i, j: (i, j, 0)),
        out_shape=jax.ShapeDtypeStruct((b, s, NA_WIDTH), BF),
        compiler_params=_params(("parallel", "parallel")), name="na_attn",
    )(qkv, qkv, qkv, bias_tab)


def _na_bias_table(rpb):
    qc = np.arange(GRID_W)[:, None]
    kc = np.arange(GRID_W)[None, :]
    win_start = np.clip(qc - NA_COLS // 2, 0, GRID_W - NA_COLS)
    valid = (kc >= win_start) & (kc < win_start + NA_COLS)
    dc = np.clip(kc - qc + NA_COLS - 1, 0, 2 * NA_COLS - 2)
    tab = jnp.where(jnp.asarray(valid)[None, None], rpb[:, :, dc], MASK_VALUE)
    t = jnp.stack([tab[:, d:d + NA_ROWS] for d in range(NA_ROWS)], axis=0)
    t = t.transpose(0, 1, 3, 2, 4)
    return t.reshape(NA_ROWS, NA_HEADS * GRID_W, NA_ROWS * GRID_W).astype(F32)


def _mla_kernel(q_ref, k_ref, vt_ref, o_ref, s0_ref, s1_ref, mx_ref, acc_ref, m_ref, *, tk, unroll):
    qt = q_ref[0, 0]
    tq = qt.shape[1]
    nk = k_ref.shape[2] // tk
    s_refs = (s0_ref, s1_ref)

    def key_slice(c):
        return pl.ds(pl.multiple_of(c * tk, tk), tk)

    def scores(c, slot):
        st = _dot(k_ref[0, 0, key_slice(c), :], qt)
        s_refs[slot][...] = st
        mx_ref[slot] = jnp.max(st, axis=0, keepdims=True)

    def consume(c, slot):
        s_ref = s_refs[slot]
        vt = vt_ref[0, 0, :, key_slice(c)]
        for j in range(tq // MXU_COLS):
            cs = slice(j * MXU_COLS, (j + 1) * MXU_COLS)
            m = m_ref[:, cs]
            m_new = jnp.maximum(m, mx_ref[slot, :, cs])
            p = jnp.exp2(s_ref[:, cs] - m_new).astype(BF)
            acc_ref[:, cs] = acc_ref[:, cs] * jnp.exp2(m - m_new) + _dot(vt, p)
            m_ref[:, cs] = m_new

    scores(0, 0)
    m_ref[...] = jnp.full(m_ref.shape, MASK_VALUE, F32)
    acc_ref[...] = jnp.zeros(acc_ref.shape, F32)

    def body(i, carry):
        c0 = unroll * i
        for u in range(unroll):
            scores(c0 + u + 1, (u + 1) % 2)
            consume(c0 + u, u % 2)
        return carry

    lax.fori_loop(0, nk // unroll - 1, body, 0)
    c0 = nk - unroll
    for u in range(unroll):
        if u + 1 < unroll:
            scores(c0 + u + 1, (u + 1) % 2)
        consume(c0 + u, u % 2)
    acc = acc_ref[...]
    o_ref[0, 0] = (acc[0:MLA_V] / acc[MLA_V:MLA_V + 1]).astype(BF)


def _mla_call(q, k, vt, tq, tk, unroll):
    b, h, s, _ = k.shape
    assert unroll % 2 == 0 and (s // tk) % unroll == 0
    return pl.pallas_call(
        functools.partial(_mla_kernel, tk=tk, unroll=unroll),
        grid=(b, h, s // tq),
        in_specs=[pl.BlockSpec((1, 1, HEAD_PAD, tq), lambda i, j, t: (i, j, 0, t)),
                  pl.BlockSpec((1, 1, s, HEAD_PAD), lambda i, j, t: (i, j, 0, 0)),
                  pl.BlockSpec((1, 1, VT_ROWS, s), lambda i, j, t: (i, j, 0, 0))],
        out_specs=pl.BlockSpec((1, 1, MLA_V, tq), lambda i, j, t: (i, j, 0, t)),
        out_shape=jax.ShapeDtypeStruct((b, h, MLA_V, s), BF),
        scratch_shapes=[pltpu.VMEM((tk, tq), F32), pltpu.VMEM((tk, tq), F32), pltpu.VMEM((2, 1, tq), F32),
                        pltpu.VMEM((VT_ROWS, tq), F32), pltpu.VMEM((1, tq), F32)],
        compiler_params=_params(("parallel", "parallel", "parallel")), name="mla_attn",
    )(q, k, vt)


def _top2_route(x, wh, wl, bias):
    xh = x.astype(BF)
    xl = (x - xh.astype(F32)).astype(BF)
    lg = _dot_nt(wh, xh) + _dot_nt(wh, xl) + _dot_nt(wl, xh) + bias
    rid = lax.broadcasted_iota(jnp.int32, lg.shape, 0)
    pad = lg.shape[0]
    m1 = jnp.max(lg, axis=0, keepdims=True)
    i1 = jnp.min(jnp.where(lg == m1, rid, pad), axis=0, keepdims=True)
    lg2 = jnp.where(rid == i1, MASK_VALUE, lg)
    m2 = jnp.max(lg2, axis=0, keepdims=True)
    i2 = jnp.min(jnp.where(lg2 == m2, rid, pad), axis=0, keepdims=True)
    e = jnp.exp(m2 - m1)
    return jnp.concatenate([i1.astype(F32), i2.astype(F32), 1.0 / (1.0 + e), e / (1.0 + e),
                            jnp.zeros((4, lg.shape[1]), F32)], axis=0)


def _merge_kernel(x_ref, ona_ref, omt_ref, gate_ref, wna_ref, wmla_ref, wout_ref, g_ref, b_ref, *rest,
                  alpha, route):
    d = x_ref.shape[2]
    a = _dot(ona_ref[0], wna_ref[...])
    m = _dot_tn(omt_ref[0], wmla_ref[...])
    gate = gate_ref[0]
    merged = gate[:, :d].astype(F32) * a + gate[:, d:].astype(F32) * m
    y = alpha * x_ref[0] + _dot(merged.astype(BF), wout_ref[...])
    out = _layer_norm(y, g_ref[...], b_ref[...])
    if route:
        wh_ref, wl_ref, br_ref, o_ref, ob_ref, r_ref = rest
        ob_ref[0] = out.astype(BF)
        r_ref[...] = _top2_route(out, wh_ref[...], wl_ref[...], br_ref[...])
    else:
        (o_ref,) = rest
    o_ref[0] = out


def _merge_call(x, o_na, o_mla_t, gates, w, alpha, tm, route):
    b, s, d = x.shape
    nt = s // tm
    weights = [w['wnao'], w['wmlao'], w['wout'], w['ln1g'], w['ln1b']]
    tok = pl.BlockSpec((1, tm, d), lambda i, j: (i, j, 0))
    out_specs, out_shape = [tok], [jax.ShapeDtypeStruct((b, s, d), F32)]
    if route:
        weights += [w['wrh'], w['wrl'], w['br']]
        out_specs += [tok, pl.BlockSpec((8, tm), lambda i, j: (0, i * nt + j))]
        out_shape += [jax.ShapeDtypeStruct((b, s, d), BF), jax.ShapeDtypeStruct((8, b * s), F32)]
    in_specs = [tok,
                pl.BlockSpec((1, tm, NA_WIDTH), lambda i, j: (i, j, 0)),
                pl.BlockSpec((1, MLA_HEADS * MLA_V, tm), lambda i, j: (i, 0, j)),
                pl.BlockSpec((1, tm, 2 * d), lambda i, j: (i, j, 0))] + [_const_spec(a.shape) for a in weights]
    return pl.pallas_call(
        functools.partial(_merge_kernel, alpha=alpha, route=route),
        grid=(b, nt), in_specs=in_specs, out_specs=tuple(out_specs), out_shape=tuple(out_shape),
        compiler_params=_params(("parallel", "parallel")), name="merge_ln1",
    )(x, o_na, o_mla_t, gates, *weights)


def _ple_ln2(x, xb, f, p_ref, wpg_ref, wp_ref, g_ref, b_ref, alpha):
    ple = jax.nn.sigmoid(_dot(xb, wpg_ref[...])) * _dot(p_ref[0].astype(BF), wp_ref[...])
    return _layer_norm(alpha * x + f + ple, g_ref[...], b_ref[...])


def _ffn_kernel(x_ref, p_ref, w1_ref, w3_ref, w2_ref, wpg_ref, wp_ref, g_ref, b_ref, o_ref,
                *, alpha, chunks):
    x = x_ref[0]
    xb = x.astype(BF)
    fc = w1_ref.shape[1] // chunks
    f = None
    for c in range(chunks):
        sl = slice(c * fc, (c + 1) * fc)
        h = _silu(_dot(xb, w1_ref[:, sl])) * _dot(xb, w3_ref[:, sl])
        part = _dot(h.astype(BF), w2_ref[sl, :])
        f = part if f is None else f + part
    o_ref[0] = _ple_ln2(x, xb, f, p_ref, wpg_ref, wp_ref, g_ref, b_ref, alpha)


def _ffn_call(x, p, w, alpha, tm, chunks):
    b, s, d = x.shape
    weights = [w['w1'], w['w3'], w['w2'], w['wpg'], w['wp'], w['ln2g'], w['ln2b']]
    in_specs = [pl.BlockSpec((1, tm, d), lambda i, j: (i, j, 0)),
                pl.BlockSpec((1, tm, p.shape[-1]), lambda i, j: (i, j, 0))] + [_const_spec(a.shape) for a in weights]
    return pl.pallas_call(
        functools.partial(_ffn_kernel, alpha=alpha, chunks=chunks),
        grid=(b, s // tm), in_specs=in_specs,
        out_specs=pl.BlockSpec((1, tm, d), lambda i, j: (i, j, 0)),
        out_shape=jax.ShapeDtypeStruct((b, s, d), F32),
        compiler_params=_params(("parallel", "parallel")), name="ffn_ple_ln2",
    )(x, p, *weights)


def _experts_kernel(te_ref, na_ref, xs_ref, w1_ref, w3_ref, w2_ref, o_ref, *, chunks):
    active = pl.program_id(0) < na_ref[0]

    @pl.when(active)
    def _():
        xb = xs_ref[...]
        fc = w1_ref.shape[2] // chunks
        acc = None
        for c in range(chunks):
            sl = slice(c * fc, (c + 1) * fc)
            h = _silu(_dot(xb, w1_ref[0, :, sl])) * _dot(xb, w3_ref[0, :, sl])
            part = _dot(h.astype(BF), w2_ref[0, sl, :])
            acc = part if acc is None else acc + part
        o_ref[...] = acc.astype(BF)

    @pl.when(jnp.logical_not(active))
    def _():
        o_ref[...] = jnp.zeros_like(o_ref)


def _experts_call(tile_expert, n_active, xs, w1, w3, w2, tm, chunks):
    p_rows, d = xs.shape
    dff = w1.shape[2]

    def expert_spec(shape):
        return pl.BlockSpec(shape, lambda t, te, na: (te[t], 0, 0), pipeline_mode=pl.Buffered(1))

    grid_spec = pltpu.PrefetchScalarGridSpec(
        num_scalar_prefetch=2, grid=(p_rows // tm,),
        in_specs=[pl.BlockSpec((tm, d), lambda t, te, na: (t, 0)),
                  expert_spec((1, d, dff)), expert_spec((1, d, dff)), expert_spec((1, dff, d))],
        out_specs=pl.BlockSpec((tm, d), lambda t, te, na: (t, 0)))
    return pl.pallas_call(
        functools.partial(_experts_kernel, chunks=chunks), grid_spec=grid_spec,
        out_shape=jax.ShapeDtypeStruct((p_rows, d), BF),
        compiler_params=_params(("arbitrary",)), name="experts",
    )(tile_expert, n_active, xs, w1, w3, w2)


def _combine_kernel(x_ref, ya_ref, yb_ref, gw_ref, p_ref, wpg_ref, wp_ref, g_ref, b_ref, o_ref, *, alpha):
    x = x_ref[0]
    gw = gw_ref[0]
    f = gw[:, 0:1] * ya_ref[0, 0].astype(F32) + gw[:, 1:2] * yb_ref[0, 0].astype(F32)
    o_ref[0] = _ple_ln2(x, x.astype(BF), f, p_ref, wpg_ref, wp_ref, g_ref, b_ref, alpha)


def _combine_call(x, y2, gw, p, w, alpha, tm):
    b, s, d = x.shape
    weights = [w['wpg'], w['wp'], w['ln2g'], w['ln2b']]
    tok = lambda width: pl.BlockSpec((1, tm, width), lambda i, j: (i, j, 0))
    choice = lambda k: pl.BlockSpec((1, 1, tm, d), lambda i, j: (k, i, j, 0))
    in_specs = ([tok(d), choice(0), choice(1), tok(2), tok(p.shape[-1])]
                + [_const_spec(a.shape) for a in weights])
    return pl.pallas_call(
        functools.partial(_combine_kernel, alpha=alpha),
        grid=(b, s // tm), in_specs=in_specs, out_specs=tok(d),
        out_shape=jax.ShapeDtypeStruct((b, s, d), F32),
        compiler_params=_params(("parallel", "parallel")), name="moe_combine_ple_ln2",
    )(x, y2, y2, gw, p, *weights)


def _moe_layer(x1, x1b, route, p, w, alpha, tl):
    b, s, d = x1.shape
    n = b * s
    tm = tl['moe_tm']
    e_all = route[0:2].astype(jnp.int32).reshape(2 * n)
    gw = route[2:4].T.reshape(b, s, 2)

    onehot = (e_all[:, None] == jnp.arange(N_EXPERTS, dtype=jnp.int32)[None, :]).astype(jnp.int32)
    csum = jnp.cumsum(onehot, axis=0)
    rank = jnp.sum(onehot * (csum - 1), axis=1)
    counts = csum[-1]
    padded = ((counts + tm - 1) // tm) * tm
    ends = jnp.cumsum(padded)
    starts = ends - padded
    dest = starts[e_all] + rank
    p_rows = 2 * n + N_EXPERTS * tm
    n_active = (ends[-1] // tm).astype(jnp.int32).reshape(1)
    tile_start = jnp.minimum(jnp.arange(p_rows // tm, dtype=jnp.int32), n_active[0] - 1) * tm
    tile_expert = jnp.sum((tile_start[:, None] >= ends[None, :]).astype(jnp.int32), axis=1)
    tile_expert = jnp.minimum(tile_expert, N_EXPERTS - 1).astype(jnp.int32)
    order = jnp.argsort(e_all, stable=True).astype(jnp.int32)
    row_expert = jnp.repeat(tile_expert, tm)
    row_rank = jnp.arange(p_rows, dtype=jnp.int32) - starts[row_expert]
    src = jnp.clip((jnp.cumsum(counts) - counts)[row_expert] + row_rank, 0, 2 * n - 1)
    row_token = jnp.where(row_rank < counts[row_expert], order[src] % n, 0)

    xs = x1b.reshape(n, d).at[row_token].get(mode='promise_in_bounds')
    ys = _experts_call(tile_expert, n_active, xs, w['mw1'], w['mw3'], w['mw2'], tm, tl['moe_chunks'])
    y2 = ys.at[dest].get(mode='promise_in_bounds').reshape(2, b, s, d)
    return _combine_call(x1, y2, gw, p, w, alpha, tl['tm'])


def _rope_tables(s):
    pos = jnp.arange(s, dtype=F32)
    inv_freq = ROPE_THETA ** (-jnp.arange(0, MLA_ROPE // 2, dtype=F32) * (2.0 / MLA_ROPE))
    ang = pos[:, None] * inv_freq[None, :]
    cos, sin = jnp.cos(ang), jnp.sin(ang)
    ck = jnp.concatenate([jnp.ones((s, MLA_NOPE), F32), cos, cos,
                          jnp.zeros((s, HEAD_PAD - MLA_NOPE - MLA_ROPE), F32)], axis=1)
    sk = jnp.concatenate([jnp.zeros((s, MLA_NOPE), F32), sin, sin,
                          jnp.zeros((s, HEAD_PAD - MLA_NOPE - MLA_ROPE), F32)], axis=1)
    qs = (MLA_NOPE + MLA_ROPE) ** -0.5 * math.log2(math.e)
    return (ck * qs).T, (sk * qs).T, ck, sk


def _rotate_half_cols(w_rope):
    half = w_rope.shape[-1] // 2
    return jnp.concatenate([-w_rope[..., half:], w_rope[..., :half]], axis=-1)


def _prep_layer(i, w_in, b_gate, q_norm_g, w_uq, kv_norm_g, w_ukv, w_na_o, w_mla_o, w_out, ln1_g, ln1_b,
                w_ple_gate, w_ple, ln2_g, ln2_b):
    d = w_in.shape[1]
    o = 0
    wna = w_in[i, :, o:o + 3 * NA_WIDTH]; o += 3 * NA_WIDTH
    wql = w_in[i, :, o:o + MLA_Q_LORA]; o += MLA_Q_LORA
    wkvl = w_in[i, :, o:o + MLA_KV_LORA]; o += MLA_KV_LORA
    wkr = w_in[i, :, o:o + MLA_ROPE]; o += MLA_ROPE
    wg = w_in[i, :, o:o + 2 * d]
    na_scale = jnp.concatenate([jnp.full((NA_WIDTH,), NA_HEAD_DIM ** -0.5, F32), jnp.ones((2 * NA_WIDTH,), F32)])
    wna = wna * na_scale[None, :]

    zpad = HEAD_PAD - MLA_NOPE - MLA_ROPE
    uq = w_uq[i].reshape(MLA_Q_LORA, MLA_HEADS, MLA_NOPE + MLA_ROPE)
    assert zpad == MLA_ROPE
    uq_pad = jnp.concatenate([uq, _rotate_half_cols(uq[..., MLA_NOPE:])], axis=-1)
    kr_pad = jnp.pad(wkr, ((0, 0), (MLA_NOPE, zpad)))
    kr_rot = jnp.pad(_rotate_half_cols(wkr), ((0, 0), (MLA_NOPE, zpad)))
    ukv = w_ukv[i].reshape(MLA_KV_LORA, MLA_HEADS, MLA_NOPE + MLA_V)
    uk_pad = jnp.pad(ukv[..., :MLA_NOPE], ((0, 0), (0, 0), (0, HEAD_PAD - MLA_NOPE)))
    uvt = ukv[..., MLA_NOPE:].reshape(MLA_KV_LORA, MLA_HEADS * MLA_V).T

    return dict(
        wna=wna.astype(BF), wg=wg.astype(BF), bg=b_gate[i][None, :], wql=wql.astype(BF), wkvl=wkvl.astype(BF),
        wkr=jnp.concatenate([kr_pad, kr_rot], axis=1).astype(BF),
        gq=q_norm_g[i][None, :], gkv=kv_norm_g[i][None, :],
        wuq=uq_pad.reshape(MLA_Q_LORA, MLA_HEADS * HEAD_PAD).T.astype(BF),
        wuk=uk_pad.reshape(MLA_KV_LORA, MLA_HEADS * HEAD_PAD).astype(BF), wuvt=uvt.astype(BF),
        wnao=w_na_o[i].astype(BF), wmlao=w_mla_o[i].astype(BF), wout=w_out[i].astype(BF),
        ln1g=ln1_g[i][None, :], ln1b=ln1_b[i][None, :],
        wpg=w_ple_gate[i].astype(BF), wp=w_ple[i].astype(BF), ln2g=ln2_g[i][None, :], ln2b=ln2_b[i][None, :],
    )


def kernel(x, p, w_in, b_gate, q_norm_g, w_uq, kv_norm_g, w_ukv, na_rpb, w_na_o, w_mla_o, w_out, ln1_g, ln1_b,
           ffn_w1, ffn_w3, ffn_w2, moe_w_router, moe_b_router, moe_w1, moe_w3, moe_w2, w_ple_gate, w_ple,
           ln2_g, ln2_b):
    b, s, d = x.shape
    depth = w_in.shape[0]
    alpha = (2 * depth) ** 0.25
    tl = _tiles(b * s, s)
    tabs = _rope_tables(s)

    for i in range(depth):
        w = _prep_layer(i, w_in, b_gate, q_norm_g, w_uq, kv_norm_g, w_ukv, w_na_o, w_mla_o, w_out, ln1_g, ln1_b,
                        w_ple_gate, w_ple, ln2_g, ln2_b)
        qkv, gates, q_mla, k_mla, vt = _proj_call(x, w, tabs, tl['tm'])
        o_na = _na_call(qkv, _na_bias_table(na_rpb[i]), tl['na_rows'])
        o_mla_t = _mla_call(q_mla, k_mla, vt, tl['tq'], tl['tk'], tl['mla_unroll'])
        o_mla_t = o_mla_t.reshape(b, MLA_HEADS * MLA_V, s)
        j = i // 2
        if i % 2 == 0:
            (x1,) = _merge_call(x, o_na, o_mla_t, gates, w, alpha, tl['tm'], route=False)
            w.update(w1=ffn_w1[j].astype(BF), w3=ffn_w3[j].astype(BF), w2=ffn_w2[j].astype(BF))
            x = _ffn_call(x1, p[i], w, alpha, tl['tm'], tl['ffn_chunks'])
        else:
            wr = jnp.pad(moe_w_router[j].T, ((0, BF16_SUBLANES - N_EXPERTS), (0, 0)))
            wrh = wr.astype(BF)
            br = jnp.concatenate([moe_b_router[j], jnp.full((BF16_SUBLANES - N_EXPERTS,), MASK_VALUE, F32)])
            w.update(wrh=wrh, wrl=(wr - wrh.astype(F32)).astype(BF), br=br[:, None])
            x1, x1b, route = _merge_call(x, o_na, o_mla_t, gates, w, alpha, tl['tm'], route=True)
            anchor = route[4, 0]
            w.update(mw1=(moe_w1[j] + anchor).astype(BF), mw3=(moe_w3[j] + anchor).astype(BF),
                     mw2=(moe_w2[j] + anchor).astype(BF))
            x = _moe_layer(x1, x1b, route, p[i], w, alpha, tl)
    return x
```

```python
import functools
import math

import numpy as np
import jax
import jax.numpy as jnp
from jax import lax
from jax.experimental import pallas as pl
from jax.experimental.pallas import tpu as pltpu

BF = jnp.bfloat16
F32 = jnp.float32

GRID_W = 64
NA_HEADS = 8
NA_HEAD_DIM = 64
NA_ROWS = 8
NA_COLS = 16
NA_WIDTH = NA_HEADS * NA_HEAD_DIM
MLA_HEADS = 8
MLA_NOPE = 64
MLA_ROPE = 32
MLA_V = 64
MLA_Q_LORA = 768
MLA_KV_LORA = 256
ROPE_THETA = 10000.0
N_EXPERTS = 8
LN_EPS = 1e-5
RMS_EPS = 1e-6
MASK_VALUE = -1e30

LANES = 128
BF16_SUBLANES = 16
MXU_COLS = 256
VMEM_LIMIT_BYTES = 56 * 1024 * 1024

HEAD_PAD = LANES
VT_ROWS = MLA_V + BF16_SUBLANES


def _dot(a, b):
    return jnp.dot(a, b, preferred_element_type=F32)


def _dot_nt(a, b):
    return lax.dot_general(a, b, (((1,), (1,)), ((), ())), preferred_element_type=F32)


def _dot_tn(a, b):
    return lax.dot_general(a, b, (((0,), (0,)), ((), ())), preferred_element_type=F32)


def _const_spec(shape):
    nd = len(shape)
    return pl.BlockSpec(shape, lambda *_: (0,) * nd, pipeline_mode=pl.Buffered(1))


def _params(sem):
    return pltpu.CompilerParams(dimension_semantics=sem, vmem_limit_bytes=VMEM_LIMIT_BYTES)


def _tiles(n_tokens, seq):
    return dict(
        tm=min(512, seq),
        tq=min(1024, seq),
        tk=min(512, seq),
        mla_unroll=4 if (seq // min(512, seq)) % 4 == 0 else 2,
        na_rows=8,
        moe_tm=512 if n_tokens >= 8192 else 128,
        moe_chunks=2,
        ffn_chunks=1,
    )


def _layer_norm(y, g, b):
    mu = jnp.mean(y, axis=-1, keepdims=True)
    yc = y - mu
    var = jnp.mean(yc * yc, axis=-1, keepdims=True)
    return yc * lax.rsqrt(var + LN_EPS) * g + b


def _rms_norm(y, g):
    return y * lax.rsqrt(jnp.mean(y * y, axis=-1, keepdims=True) + RMS_EPS) * g


def _silu(y):
    return y * jax.nn.sigmoid(y)


def _proj_kernel(x_ref, wna_ref, wg_ref, bg_ref, wql_ref, wkvl_ref, wkr_ref, gq_ref, gkv_ref,
                 wuq_ref, wuk_ref, wuvt_ref, cq_ref, sq_ref, ck_ref, sk_ref,
                 qkv_ref, gate_ref, q_ref, k_ref, vt_ref):
    xb = x_ref[0].astype(BF)
    qkv_ref[0] = _dot(xb, wna_ref[...]).astype(BF)
    gate_ref[0] = jax.nn.sigmoid(_dot(xb, wg_ref[...]) + bg_ref[...]).astype(BF)

    qn = _rms_norm(_dot(xb, wql_ref[...]), gq_ref[...]).astype(BF)
    qat = _dot_nt(wuq_ref[...], qn)
    cq, sq = cq_ref[...], sq_ref[...]
    for h in range(MLA_HEADS):
        qh = qat[h * HEAD_PAD:(h + 1) * HEAD_PAD]
        partner = pltpu.roll(qh, shift=HEAD_PAD - MLA_ROPE, axis=0)
        q_ref[0, h] = (qh * cq + partner * sq).astype(BF)

    kvn = _rms_norm(_dot(xb, wkvl_ref[...]), gkv_ref[...]).astype(BF)
    kr2 = _dot(xb, wkr_ref[...])
    kr = kr2[:, :HEAD_PAD] * ck_ref[...] + kr2[:, HEAD_PAD:] * sk_ref[...]
    kn = _dot(kvn, wuk_ref[...])
    for h in range(MLA_HEADS):
        sl = slice(h * HEAD_PAD, (h + 1) * HEAD_PAD)
        k_ref[0, h] = (kn[:, sl] + kr).astype(BF)

    vt = _dot_nt(wuvt_ref[...], kvn)
    ones = jnp.ones((BF16_SUBLANES, vt.shape[1]), BF)
    for h in range(MLA_HEADS):
        vt_ref[0, h, 0:MLA_V, :] = vt[h * MLA_V:(h + 1) * MLA_V].astype(BF)
        vt_ref[0, h, MLA_V:VT_ROWS, :] = ones


def _proj_call(x, w, tabs, tm):
    b, s, d = x.shape
    grid = (b, s // tm)
    weights = [w['wna'], w['wg'], w['bg'], w['wql'], w['wkvl'], w['wkr'], w['gq'], w['gkv'],
               w['wuq'], w['wuk'], w['wuvt']]
    tab_spec = pl.BlockSpec((tm, HEAD_PAD), lambda i, j: (j, 0))
    tab_t_spec = pl.BlockSpec((HEAD_PAD, tm), lambda i, j: (0, j))
    in_specs = ([pl.BlockSpec((1, tm, d), lambda i, j: (i, j, 0))]
                + [_const_spec(a.shape) for a in weights] + [tab_t_spec] * 2 + [tab_spec] * 2)
    out_shape = (
        jax.ShapeDtypeStruct((b, s, 3 * NA_WIDTH), BF),
        jax.ShapeDtypeStruct((b, s, 2 * d), BF),
        jax.ShapeDtypeStruct((b, MLA_HEADS, HEAD_PAD, s), BF),
        jax.ShapeDtypeStruct((b, MLA_HEADS, s, HEAD_PAD), BF),
        jax.ShapeDtypeStruct((b, MLA_HEADS, VT_ROWS, s), BF),
    )
    out_specs = (
        pl.BlockSpec((1, tm, 3 * NA_WIDTH), lambda i, j: (i, j, 0)),
        pl.BlockSpec((1, tm, 2 * d), lambda i, j: (i, j, 0)),
        pl.BlockSpec((1, MLA_HEADS, HEAD_PAD, tm), lambda i, j: (i, 0, 0, j)),
        pl.BlockSpec((1, MLA_HEADS, tm, HEAD_PAD), lambda i, j: (i, 0, j, 0)),
        pl.BlockSpec((1, MLA_HEADS, VT_ROWS, tm), lambda i, j: (i, 0, 0, j)),
    )
    return pl.pallas_call(
        _proj_kernel, grid=grid, in_specs=in_specs, out_specs=out_specs, out_shape=out_shape,
        compiler_params=_params(("parallel", "parallel")), name="proj",
    )(x, *weights, *tabs)


def _na_halo_start(i, rows, rb):
    return jnp.clip(i * rb - NA_ROWS // 2, 0, rows - (rb + NA_ROWS - 1))


def _na_kernel(q_ref, kwin_ref, vwin_ref, bias_ref, o_ref, *, rows, rb):
    i = pl.program_id(1)
    win = NA_ROWS * GRID_W
    base = _na_halo_start(i, rows, rb)
    kcat_ref, vcat_ref = kwin_ref.at[0], vwin_ref.at[0]
    lane = lax.broadcasted_iota(jnp.int32, (GRID_W, LANES), 1)
    first_head = lane < NA_HEAD_DIM

    def row(a, carry):
        r = i * rb + a
        rs = jnp.clip(r - NA_ROWS // 2, 0, rows - NA_ROWS)
        off = pl.multiple_of((rs - base) * GRID_W, GRID_W)
        d = rs - r + (NA_ROWS - 1)
        qoff = pl.multiple_of(a * GRID_W, GRID_W)
        qrow = q_ref[0, pl.ds(qoff, GRID_W), :]
        zero = jnp.zeros((GRID_W, LANES), BF)
        sc = []
        for hp in range(NA_HEADS // 2):
            sl = slice(hp * LANES, (hp + 1) * LANES)
            qp = qrow[:, sl]
            qm = jnp.concatenate([jnp.where(first_head, qp, zero), jnp.where(first_head, zero, qp)], axis=0)
            sc.append(_dot_nt(qm, kcat_ref[pl.ds(off, win), sl]))
        sc = jnp.concatenate(sc, axis=0) + bias_ref[d]
        m = jnp.max(sc, axis=-1, keepdims=True)
        p = jnp.exp(sc - m)
        inv_l = 1.0 / jnp.sum(p, axis=-1, keepdims=True)
        pb = p.astype(BF)
        for hp in range(NA_HEADS // 2):
            sl = slice(hp * LANES, (hp + 1) * LANES)
            rows2 = slice(hp * 2 * GRID_W, (hp + 1) * 2 * GRID_W)
            o2 = _dot(pb[rows2], vcat_ref[pl.ds(off, win), sl]) * inv_l[rows2]
            o_ref[0, pl.ds(qoff, GRID_W), sl] = jnp.where(first_head, o2[:GRID_W], o2[GRID_W:]).astype(BF)
        return carry

    lax.fori_loop(0, rb, row, 0, unroll=True)


def _na_call(qkv, bias_tab, rb):
    b, s, _ = qkv.shape
    rows = s // GRID_W
    nb = rows // rb
    blk = rb * GRID_W

    halo_rows = rb + NA_ROWS - 1
    assert rows >= halo_rows and rows % rb == 0

    def halo_spec(col):
        return pl.BlockSpec((pl.Element(1), pl.Element(halo_rows * GRID_W), pl.Element(NA_WIDTH)),
                            lambda i, j: (i, _na_halo_start(j, rows, rb) * GRID_W, col * NA_WIDTH))

    in_specs = [pl.BlockSpec((1, blk, NA_WIDTH), lambda i, j: (i, j, 0)),
                halo_spec(1), halo_spec(2), _const_spec(bias_tab.shape)]
    return pl.pallas_call(
        functools.partial(_na_kernel, rows=rows, rb=rb),
        grid=(b, nb), in_specs=in_specs,
        out_specs=pl.BlockSpec((1, blk, NA_WIDTH), lambda i, j: (i, j, 0)),
        out_shape=jax.ShapeDtypeStruct((b, s, NA_WIDTH), BF),
        compiler_params=_params(("parallel", "parallel")), name="na_attn",
    )(qkv, qkv, qkv, bias_tab)


def _na_bias_table(rpb):
    qc = np.arange(GRID_W)[:, None]
    kc = np.arange(GRID_W)[None, :]
    win_start = np.clip(qc - NA_COLS // 2, 0, GRID_W - NA_COLS)
    valid = (kc >= win_start) & (kc < win_start + NA_COLS)
    dc = np.clip(kc - qc + NA_COLS - 1, 0, 2 * NA_COLS - 2)
    tab = jnp.where(jnp.asarray(valid)[None, None], rpb[:, :, dc], MASK_VALUE)
    t = jnp.stack([tab[:, d:d + NA_ROWS] for d in range(NA_ROWS)], axis=0)
    t = t.transpose(0, 1, 3, 2, 4)
    return t.reshape(NA_ROWS, NA_HEADS * GRID_W, NA_ROWS * GRID_W).astype(F32)


def _mla_kernel(q_ref, k_ref, vt_ref, o_ref, s0_ref, s1_ref, mx_ref, acc_ref, m_ref, *, tk, unroll):
    qt = q_ref[0, 0]
    tq = qt.shape[1]
    nk = k_ref.shape[2] // tk
    s_refs = (s0_ref, s1_ref)

    def key_slice(c):
        return pl.ds(pl.multiple_of(c * tk, tk), tk)

    def scores(c, slot):
        st = _dot(k_ref[0, 0, key_slice(c), :], qt)
        s_refs[slot][...] = st
        mx_ref[slot] = jnp.max(st, axis=0, keepdims=True)

    def consume(c, slot):
        s_ref = s_refs[slot]
        vt = vt_ref[0, 0, :, key_slice(c)]
        for j in range(tq // MXU_COLS):
            cs = slice(j * MXU_COLS, (j + 1) * MXU_COLS)
            m = m_ref[:, cs]
            m_new = jnp.maximum(m, mx_ref[slot, :, cs])
            p = jnp.exp2(s_ref[:, cs] - m_new).astype(BF)
            acc_ref[:, cs] = acc_ref[:, cs] * jnp.exp2(m - m_new) + _dot(vt, p)
            m_ref[:, cs] = m_new

    scores(0, 0)
    m_ref[...] = jnp.full(m_ref.shape, MASK_VALUE, F32)
    acc_ref[...] = jnp.zeros(acc_ref.shape, F32)

    def body(i, carry):
        c0 = unroll * i
        for u in range(unroll):
            scores(c0 + u + 1, (u + 1) % 2)
            consume(c0 + u, u % 2)
        return carry

    lax.fori_loop(0, nk // unroll - 1, body, 0)
    c0 = nk - unroll
    for u in range(unroll):
        if u + 1 < unroll:
            scores(c0 + u + 1, (u + 1) % 2)
        consume(c0 + u, u % 2)
    acc = acc_ref[...]
    o_ref[0, 0] = (acc[0:MLA_V] / acc[MLA_V:MLA_V + 1]).astype(BF)


def _mla_call(q, k, vt, tq, tk, unroll):
    b, h, s, _ = k.shape
    assert unroll % 2 == 0 and (s // tk) % unroll == 0
    return pl.pallas_call(
        functools.partial(_mla_kernel, tk=tk, unroll=unroll),
        grid=(b, h, s // tq),
        in_specs=[pl.BlockSpec((1, 1, HEAD_PAD, tq), lambda i, j, t: (i, j, 0, t)),
                  pl.BlockSpec((1, 1, s, HEAD_PAD), lambda i, j, t: (i, j, 0, 0)),
                  pl.BlockSpec((1, 1, VT_ROWS, s), lambda i, j, t: (i, j, 0, 0))],
        out_specs=pl.BlockSpec((1, 1, MLA_V, tq), lambda i, j, t: (i, j, 0, t)),
        out_shape=jax.ShapeDtypeStruct((b, h, MLA_V, s), BF),
        scratch_shapes=[pltpu.VMEM((tk, tq), F32), pltpu.VMEM((tk, tq), F32), pltpu.VMEM((2, 1, tq), F32),
                        pltpu.VMEM((VT_ROWS, tq), F32), pltpu.VMEM((1, tq), F32)],
        compiler_params=_params(("parallel", "parallel", "parallel")), name="mla_attn",
    )(q, k, vt)


def _top2_route(x, wh, wl, bias):
    xh = x.astype(BF)
    xl = (x - xh.astype(F32)).astype(BF)
    lg = _dot_nt(wh, xh) + _dot_nt(wh, xl) + _dot_nt(wl, xh) + bias
    rid = lax.broadcasted_iota(jnp.int32, lg.shape, 0)
    pad = lg.shape[0]
    m1 = jnp.max(lg, axis=0, keepdims=True)
    i1 = jnp.min(jnp.where(lg == m1, rid, pad), axis=0, keepdims=True)
    lg2 = jnp.where(rid == i1, MASK_VALUE, lg)
    m2 = jnp.max(lg2, axis=0, keepdims=True)
    i2 = jnp.min(jnp.where(lg2 == m2, rid, pad), axis=0, keepdims=True)
    e = jnp.exp(m2 - m1)
    return jnp.concatenate([i1.astype(F32), i2.astype(F32), 1.0 / (1.0 + e), e / (1.0 + e),
                            jnp.zeros((4, lg.shape[1]), F32)], axis=0)


def _merge_kernel(x_ref, ona_ref, omt_ref, gate_ref, wna_ref, wmla_ref, wout_ref, g_ref, b_ref, *rest,
                  alpha, route):
    d = x_ref.shape[2]
    a = _dot(ona_ref[0], wna_ref[...])
    m = _dot_tn(omt_ref[0], wmla_ref[...])
    gate = gate_ref[0]
    merged = gate[:, :d].astype(F32) * a + gate[:, d:].astype(F32) * m
    y = alpha * x_ref[0] + _dot(merged.astype(BF), wout_ref[...])
    out = _layer_norm(y, g_ref[...], b_ref[...])
    if route:
        wh_ref, wl_ref, br_ref, o_ref, ob_ref, r_ref = rest
        ob_ref[0] = out.astype(BF)
        r_ref[...] = _top2_route(out, wh_ref[...], wl_ref[...], br_ref[...])
    else:
        (o_ref,) = rest
    o_ref[0] = out


def _merge_call(x, o_na, o_mla_t, gates, w, alpha, tm, route):
    b, s, d = x.shape
    nt = s // tm
    weights = [w['wnao'], w['wmlao'], w['wout'], w['ln1g'], w['ln1b']]
    tok = pl.BlockSpec((1, tm, d), lambda i, j: (i, j, 0))
    out_specs, out_shape = [tok], [jax.ShapeDtypeStruct((b, s, d), F32)]
    if route:
        weights += [w['wrh'], w['wrl'], w['br']]
        out_specs += [tok, pl.BlockSpec((8, tm), lambda i, j: (0, i * nt + j))]
        out_shape += [jax.ShapeDtypeStruct((b, s, d), BF), jax.ShapeDtypeStruct((8, b * s), F32)]
    in_specs = [tok,
                pl.BlockSpec((1, tm, NA_WIDTH), lambda i, j: (i, j, 0)),
                pl.BlockSpec((1, MLA_HEADS * MLA_V, tm), lambda i, j: (i, 0, j)),
                pl.BlockSpec((1, tm, 2 * d), lambda i, j: (i, j, 0))] + [_const_spec(a.shape) for a in weights]
    return pl.pallas_call(
        functools.partial(_merge_kernel, alpha=alpha, route=route),
        grid=(b, nt), in_specs=in_specs, out_specs=tuple(out_specs), out_shape=tuple(out_shape),
        compiler_params=_params(("parallel", "parallel")), name="merge_ln1",
    )(x, o_na, o_mla_t, gates, *weights)


def _ple_ln2(x, xb, f, p_ref, wpg_ref, wp_ref, g_ref, b_ref, alpha):
    ple = jax.nn.sigmoid(_dot(xb, wpg_ref[...])) * _dot(p_ref[0].astype(BF), wp_ref[...])
    return _layer_norm(alpha * x + f + ple, g_ref[...], b_ref[...])


def _ffn_kernel(x_ref, p_ref, w1_ref, w3_ref, w2_ref, wpg_ref, wp_ref, g_ref, b_ref, o_ref,
                *, alpha, chunks):
    x = x_ref[0]
    xb = x.astype(BF)
    fc = w1_ref.shape[1] // chunks
    f = None
    for c in range(chunks):
        sl = slice(c * fc, (c + 1) * fc)
        h = _silu(_dot(xb, w1_ref[:, sl])) * _dot(xb, w3_ref[:, sl])
        part = _dot(h.astype(BF), w2_ref[sl, :])
        f = part if f is None else f + part
    o_ref[0] = _ple_ln2(x, xb, f, p_ref, wpg_ref, wp_ref, g_ref, b_ref, alpha)


def _ffn_call(x, p, w, alpha, tm, chunks):
    b, s, d = x.shape
    weights = [w['w1'], w['w3'], w['w2'], w['wpg'], w['wp'], w['ln2g'], w['ln2b']]
    in_specs = [pl.BlockSpec((1, tm, d), lambda i, j: (i, j, 0)),
                pl.BlockSpec((1, tm, p.shape[-1]), lambda i, j: (i, j, 0))] + [_const_spec(a.shape) for a in weights]
    return pl.pallas_call(
        functools.partial(_ffn_kernel, alpha=alpha, chunks=chunks),
        grid=(b, s // tm), in_specs=in_specs,
        out_specs=pl.BlockSpec((1, tm, d), lambda i, j: (i, j, 0)),
        out_shape=jax.ShapeDtypeStruct((b, s, d), F32),
        compiler_params=_params(("parallel", "parallel")), name="ffn_ple_ln2",
    )(x, p, *weights)


def _experts_kernel(te_ref, na_ref, xs_ref, w1_ref, w3_ref, w2_ref, o_ref, *, chunks):
    active = pl.program_id(0) < na_ref[0]

    @pl.when(active)
    def _():
        xb = xs_ref[...]
        fc = w1_ref.shape[2] // chunks
        acc = None
        for c in range(chunks):
            sl = slice(c * fc, (c + 1) * fc)
            h = _silu(_dot(xb, w1_ref[0, :, sl])) * _dot(xb, w3_ref[0, :, sl])
            part = _dot(h.astype(BF), w2_ref[0, sl, :])
            acc = part if acc is None else acc + part
        o_ref[...] = acc.astype(BF)

    @pl.when(jnp.logical_not(active))
    def _():
        o_ref[...] = jnp.zeros_like(o_ref)


def _experts_call(tile_expert, n_active, xs, w1, w3, w2, tm, chunks):
    p_rows, d = xs.shape
    dff = w1.shape[2]

    def expert_spec(shape):
        return pl.BlockSpec(shape, lambda t, te, na: (te[t], 0, 0), pipeline_mode=pl.Buffered(1))

    grid_spec = pltpu.PrefetchScalarGridSpec(
        num_scalar_prefetch=2, grid=(p_rows // tm,),
        in_specs=[pl.BlockSpec((tm, d), lambda t, te, na: (t, 0)),
                  expert_spec((1, d, dff)), expert_spec((1, d, dff)), expert_spec((1, dff, d))],
        out_specs=pl.BlockSpec((tm, d), lambda t, te, na: (t, 0)))
    return pl.pallas_call(
        functools.partial(_experts_kernel, chunks=chunks), grid_spec=grid_spec,
        out_shape=jax.ShapeDtypeStruct((p_rows, d), BF),
        compiler_params=_params(("arbitrary",)), name="experts",
    )(tile_expert, n_active, xs, w1, w3, w2)


def _combine_kernel(x_ref, ya_ref, yb_ref, gw_ref, p_ref, wpg_ref, wp_ref, g_ref, b_ref, o_ref, *, alpha):
    x = x_ref[0]
    gw = gw_ref[0]
    f = gw[:, 0:1] * ya_ref[0, 0].astype(F32) + gw[:, 1:2] * yb_ref[0, 0].astype(F32)
    o_ref[0] = _ple_ln2(x, x.astype(BF), f, p_ref, wpg_ref, wp_ref, g_ref, b_ref, alpha)


def _combine_call(x, y2, gw, p, w, alpha, tm):
    b, s, d = x.shape
    weights = [w['wpg'], w['wp'], w['ln2g'], w['ln2b']]
    tok = lambda width: pl.BlockSpec((1, tm, width), lambda i, j: (i, j, 0))
    choice = lambda k: pl.BlockSpec((1, 1, tm, d), lambda i, j: (k, i, j, 0))
    in_specs = ([tok(d), choice(0), choice(1), tok(2), tok(p.shape[-1])]
                + [_const_spec(a.shape) for a in weights])
    return pl.pallas_call(
        functools.partial(_combine_kernel, alpha=alpha),
        grid=(b, s // tm), in_specs=in_specs, out_specs=tok(d),
        out_shape=jax.ShapeDtypeStruct((b, s, d), F32),
        compiler_params=_params(("parallel", "parallel")), name="moe_combine_ple_ln2",
    )(x, y2, y2, gw, p, *weights)


def _moe_layer(x1, x1b, route, p, w, alpha, tl):
    b, s, d = x1.shape
    n = b * s
    tm = tl['moe_tm']
    e_all = route[0:2].astype(jnp.int32).reshape(2 * n)
    gw = route[2:4].T.reshape(b, s, 2)

    onehot = (e_all[:, None] == jnp.arange(N_EXPERTS, dtype=jnp.int32)[None, :]).astype(jnp.int32)
    csum = jnp.cumsum(onehot, axis=0)
    rank = jnp.sum(onehot * (csum - 1), axis=1)
    counts = csum[-1]
    padded = ((counts + tm - 1) // tm) * tm
    ends = jnp.cumsum(padded)
    starts = ends - padded
    dest = starts[e_all] + rank
    p_rows = 2 * n + N_EXPERTS * tm
    n_active = (ends[-1] // tm).astype(jnp.int32).reshape(1)
    tile_start = jnp.minimum(jnp.arange(p_rows // tm, dtype=jnp.int32), n_active[0] - 1) * tm
    tile_expert = jnp.sum((tile_start[:, None] >= ends[None, :]).astype(jnp.int32), axis=1)
    tile_expert = jnp.minimum(tile_expert, N_EXPERTS - 1).astype(jnp.int32)
    order = jnp.argsort(e_all, stable=True).astype(jnp.int32)
    row_expert = jnp.repeat(tile_expert, tm)
    row_rank = jnp.arange(p_rows, dtype=jnp.int32) - starts[row_expert]
    src = jnp.clip((jnp.cumsum(counts) - counts)[row_expert] + row_rank, 0, 2 * n - 1)
    row_token = jnp.where(row_rank < counts[row_expert], order[src] % n, 0)

    xs = x1b.reshape(n, d).at[row_token].get(mode='promise_in_bounds')
    ys = _experts_call(tile_expert, n_active, xs, w['mw1'], w['mw3'], w['mw2'], tm, tl['moe_chunks'])
    y2 = ys.at[dest].get(mode='promise_in_bounds').reshape(2, b, s, d)
    return _combine_call(x1, y2, gw, p, w, alpha, tl['tm'])


def _rope_tables(s):
    pos = jnp.arange(s, dtype=F32)
    inv_freq = ROPE_THETA ** (-jnp.arange(0, MLA_ROPE // 2, dtype=F32) * (2.0 / MLA_ROPE))
    ang = pos[:, None] * inv_freq[None, :]
    cos, sin = jnp.cos(ang), jnp.sin(ang)
    ck = jnp.concatenate([jnp.ones((s, MLA_NOPE), F32), cos, cos,
                          jnp.zeros((s, HEAD_PAD - MLA_NOPE - MLA_ROPE), F32)], axis=1)
    sk = jnp.concatenate([jnp.zeros((s, MLA_NOPE), F32), sin, sin,
                          jnp.zeros((s, HEAD_PAD - MLA_NOPE - MLA_ROPE), F32)], axis=1)
    qs = (MLA_NOPE + MLA_ROPE) ** -0.5 * math.log2(math.e)
    return (ck * qs).T, (sk * qs).T, ck, sk


def _rotate_half_cols(w_rope):
    half = w_rope.shape[-1] // 2
    return jnp.concatenate([-w_rope[..., half:], w_rope[..., :half]], axis=-1)


def _prep_layer(i, w_in, b_gate, q_norm_g, w_uq, kv_norm_g, w_ukv, w_na_o, w_mla_o, w_out, ln1_g, ln1_b,
                w_ple_gate, w_ple, ln2_g, ln2_b):
    d = w_in.shape[1]
    o = 0
    wna = w_in[i, :, o:o + 3 * NA_WIDTH]; o += 3 * NA_WIDTH
    wql = w_in[i, :, o:o + MLA_Q_LORA]; o += MLA_Q_LORA
    wkvl = w_in[i, :, o:o + MLA_KV_LORA]; o += MLA_KV_LORA
    wkr = w_in[i, :, o:o + MLA_ROPE]; o += MLA_ROPE
    wg = w_in[i, :, o:o + 2 * d]
    na_scale = jnp.concatenate([jnp.full((NA_WIDTH,), NA_HEAD_DIM ** -0.5, F32), jnp.ones((2 * NA_WIDTH,), F32)])
    wna = wna * na_scale[None, :]

    zpad = HEAD_PAD - MLA_NOPE - MLA_ROPE
    uq = w_uq[i].reshape(MLA_Q_LORA, MLA_HEADS, MLA_NOPE + MLA_ROPE)
    assert zpad == MLA_ROPE
    uq_pad = jnp.concatenate([uq, _rotate_half_cols(uq[..., MLA_NOPE:])], axis=-1)
    kr_pad = jnp.pad(wkr, ((0, 0), (MLA_NOPE, zpad)))
    kr_rot = jnp.pad(_rotate_half_cols(wkr), ((0, 0), (MLA_NOPE, zpad)))
    ukv = w_ukv[i].reshape(MLA_KV_LORA, MLA_HEADS, MLA_NOPE + MLA_V)
    uk_pad = jnp.pad(ukv[..., :MLA_NOPE], ((0, 0), (0, 0), (0, HEAD_PAD - MLA_NOPE)))
    uvt = ukv[..., MLA_NOPE:].reshape(MLA_KV_LORA, MLA_HEADS * MLA_V).T

    return dict(
        wna=wna.astype(BF), wg=wg.astype(BF), bg=b_gate[i][None, :], wql=wql.astype(BF), wkvl=wkvl.astype(BF),
        wkr=jnp.concatenate([kr_pad, kr_rot], axis=1).astype(BF),
        gq=q_norm_g[i][None, :], gkv=kv_norm_g[i][None, :],
        wuq=uq_pad.reshape(MLA_Q_LORA, MLA_HEADS * HEAD_PAD).T.astype(BF),
        wuk=uk_pad.reshape(MLA_KV_LORA, MLA_HEADS * HEAD_PAD).astype(BF), wuvt=uvt.astype(BF),
        wnao=w_na_o[i].astype(BF), wmlao=w_mla_o[i].astype(BF), wout=w_out[i].astype(BF),
        ln1g=ln1_g[i][None, :], ln1b=ln1_b[i][None, :],
        wpg=w_ple_gate[i].astype(BF), wp=w_ple[i].astype(BF), ln2g=ln2_g[i][None, :], ln2b=ln2_b[i][None, :],
    )


def kernel(x, p, w_in, b_gate, q_norm_g, w_uq, kv_norm_g, w_ukv, na_rpb, w_na_o, w_mla_o, w_out, ln1_g, ln1_b,
           ffn_w1, ffn_w3, ffn_w2, moe_w_router, moe_b_router, moe_w1, moe_w3, moe_w2, w_ple_gate, w_ple,
           ln2_g, ln2_b):
    b, s, d = x.shape
    depth = w_in.shape[0]
    alpha = (2 * depth) ** 0.25
    tl = _tiles(b * s, s)
    tabs = _rope_tables(s)

    for i in range(depth):
        w = _prep_layer(i, w_in, b_gate, q_norm_g, w_uq, kv_norm_g, w_ukv, w_na_o, w_mla_o, w_out, ln1_g, ln1_b,
                        w_ple_gate, w_ple, ln2_g, ln2_b)
        qkv, gates, q_mla, k_mla, vt = _proj_call(x, w, tabs, tl['tm'])
        o_na = _na_call(qkv, _na_bias_table(na_rpb[i]), tl['na_rows'])
        o_mla_t = _mla_call(q_mla, k_mla, vt, tl['tq'], tl['tk'], tl['mla_unroll'])
        o_mla_t = o_mla_t.reshape(b, MLA_HEADS * MLA_V, s)
        j = i // 2
        if i % 2 == 0:
            (x1,) = _merge_call(x, o_na, o_mla_t, gates, w, alpha, tl['tm'], route=False)
            w.update(w1=ffn_w1[j].astype(BF), w3=ffn_w3[j].astype(BF), w2=ffn_w2[j].astype(BF))
            x = _ffn_call(x1, p[i], w, alpha, tl['tm'], tl['ffn_chunks'])
        else:
            wr = jnp.pad(moe_w_router[j].T, ((0, BF16_SUBLANES - N_EXPERTS), (0, 0)))
            wrh = wr.astype(BF)
            br = jnp.concatenate([moe_b_router[j], jnp.full((BF16_SUBLANES - N_EXPERTS,), MASK_VALUE, F32)])
            w.update(wrh=wrh, wrl=(wr - wrh.astype(F32)).astype(BF), br=br[:, None],
                     mw1=moe_w1[j].astype(BF), mw3=moe_w3[j].astype(BF), mw2=moe_w2[j].astype(BF))
            x1, x1b, route = _merge_call(x, o_na, o_mla_t, gates, w, alpha, tl['tm'], route=True)
            x = _moe_layer(x1, x1b, route, p[i], w, alpha, tl)
    return x
```
